```python
import math
import jax, jax.numpy as jnp
from jax import lax
import numpy as np

D_MODEL = 1024
BATCH = 8
SEQ = 8192
DEPTH = 2

N_META = 16
CHUNK = 128
N_PAD = (-N_META) % CHUNK

HEAD_DIM = 64
C_GRP = D_MODEL // 4
H_GRP = C_GRP // HEAD_DIM
D_MIX = 4 * C_GRP

RWKV_DECAY_LORA = 64
RWKV_AAA_LORA = 64
RWKV_GATE_LORA = 64
RWKV_LN_EPS = 64e-5
IDX_HEADS = 8
IDX_DIM = 32
INDEX_TOPK_MAX = 256
ROPE_THETA = 500000.0
ROPE_FRACTION = 4
CONV_WIDTH = 4
LRU_C = 8.0
LN_EPS = 1e-5
RMS_EPS = 1e-5
NEG_INF = -1e30
N_EXPERTS = 32
TOP_K = 4
D_EXPERT = D_MODEL
SWIGLU_ALPHA = 1.702
SWIGLU_LIMIT = 7.0
DEEPNORM_ALPHA = (2 * DEPTH) ** 0.25
DEEPNORM_BETA = (8 * DEPTH) ** -0.25

RWKV_COLS = (C_GRP, RWKV_DECAY_LORA, C_GRP, C_GRP, RWKV_AAA_LORA, RWKV_GATE_LORA)
DSA_COLS = (C_GRP, C_GRP, C_GRP, IDX_HEADS * IDX_DIM, IDX_DIM, IDX_HEADS)
LRU_COLS = (C_GRP, C_GRP)
HGRN_COLS = (C_GRP, C_GRP, C_GRP, C_GRP)
GROUP_COLS = (sum(RWKV_COLS), sum(DSA_COLS), sum(LRU_COLS), sum(HGRN_COLS))
N_IN = sum(GROUP_COLS)

kernel_name = 'hybrid_rwkv7_dsa_rglru_hgrn2_moe_trunk'


def _split(t, sizes):
    return jnp.split(t, np.cumsum(sizes)[:-1].tolist(), axis=-1)


def _layernorm(x, g, b):
    xf = x.astype(jnp.float32)
    mu = jnp.mean(xf, axis=-1, keepdims=True)
    var = jnp.mean(jnp.square(xf - mu), axis=-1, keepdims=True)
    return ((xf - mu) * lax.rsqrt(var + LN_EPS) * g + b).astype(x.dtype)


def _rope_partial(x, pos):
    d = x.shape[-1]
    rd = d // ROPE_FRACTION
    half = rd // 2
    inv_freq = ROPE_THETA ** (-jnp.arange(half, dtype=jnp.float32) / half)
    ang = pos.astype(jnp.float32)[:, None] * inv_freq[None, :]
    cos = jnp.cos(ang)[None, :, None, :]
    sin = jnp.sin(ang)[None, :, None, :]
    x1 = x[..., :half].astype(jnp.float32)
    x2 = x[..., half:rd].astype(jnp.float32)
    rot = jnp.concatenate([x1 * cos - x2 * sin, x1 * sin + x2 * cos], axis=-1).astype(x.dtype)
    return jnp.concatenate([rot, x[..., rd:]], axis=-1)


def _to_blocks(t):
    b, lp = t.shape[:2]
    return jnp.moveaxis(t.reshape((b, lp // CHUNK, CHUNK) + t.shape[2:]), 1, 0)


def _from_blocks(t):
    t = jnp.moveaxis(t, 0, 1)
    return t.reshape((t.shape[0], -1) + t.shape[3:])


def _pad_front(t):
    return jnp.pad(t, ((0, 0), (N_PAD, 0)) + ((0, 0),) * (t.ndim - 2))


def _rwkv7(p, mu, w0, w2, a0, a2, g2, k_k, k_a, r_k, ln_g, ln_b):
    f32 = jnp.float32
    B, L, _ = p.shape
    p_prev = jnp.pad(p, ((0, 0), (1, 0), (0, 0)))[:, :-1]
    p = p + (p_prev - p) * mu
    r, wd, k, v, ad, gd = _split(p, RWKV_COLS)
    w = -jax.nn.softplus(-(w0 + jnp.tanh(wd) @ w2).astype(f32)) - 0.5
    decay = jnp.exp(-jnp.exp(w))
    a = jax.nn.sigmoid((a0 + ad @ a2).astype(f32))
    g = (jax.nn.sigmoid(gd) @ g2).astype(f32)
    kf = k.astype(f32)
    kk = (kf * k_k).reshape(B, L, H_GRP, HEAD_DIM)
    kk = kk / jnp.maximum(jnp.linalg.norm(kk, axis=-1, keepdims=True), 1e-12)
    k_mod = kf * (1.0 + (a - 1.0) * k_a)
    hs = lambda t: t.astype(f32).reshape(B, L, H_GRP, HEAD_DIM)
    r_h, k_h, v_h, w_h, a_h = hs(r), hs(k_mod), hs(v), hs(decay), hs(a)
    a_vec = -kk
    b_vec = kk * a_h

    def step(S, inp):
        r_t, w_t, k_t, v_t, a_t, b_t = inp
        sa = jnp.einsum('bhvk,bhk->bhv', S, a_t)
        S = S * w_t[:, :, None, :] + sa[..., None] * b_t[:, :, None, :] + v_t[..., None] * k_t[:, :, None, :]
        return S, jnp.einsum('bhvk,bhk->bhv', S, r_t)

    xs = tuple(jnp.swapaxes(t, 0, 1) for t in (r_h, w_h, k_h, v_h, a_vec, b_vec))
    S0 = jnp.zeros((B, H_GRP, HEAD_DIM, HEAD_DIM), f32)
    _, y = lax.scan(step, S0, xs)
    y = jnp.swapaxes(y, 0, 1)
    y_mu = jnp.mean(y, axis=-1, keepdims=True)
    y_var = jnp.mean(jnp.square(y - y_mu), axis=-1, keepdims=True)
    y = ((y - y_mu) * lax.rsqrt(y_var + RWKV_LN_EPS)).reshape(B, L, C_GRP) * ln_g + ln_b
    bonus = jnp.sum(r_h * k_h * r_k, axis=-1, keepdims=True) * v_h
    y = y + bonus.reshape(B, L, C_GRP)
    return (y * g).astype(p.dtype)


def _dsa_attention(p, pos, topk):
    f32 = jnp.float32
    q, k, v, qi, ki, wi = _split(p, DSA_COLS)
    B, L, _ = q.shape
    q = _rope_partial(q.reshape(B, L, H_GRP, HEAD_DIM), pos)
    k = _rope_partial(k.reshape(B, L, H_GRP, HEAD_DIM), pos)
    v = v.reshape(B, L, H_GRP, HEAD_DIM)
    qi = _rope_partial(qi.reshape(B, L, IDX_HEADS, IDX_DIM), pos)
    ki = _rope_partial(ki[:, :, None, :], pos)[:, :, 0]
    wi = wi.astype(f32) * (IDX_HEADS ** -0.5 * IDX_DIM ** -0.5)
    q, k, v, qi, ki, wi = (_pad_front(t) for t in (q, k, v, qi, ki, wi))
    key_pos = jnp.arange(L + N_PAD, dtype=jnp.int32) - N_PAD
    ki32 = ki.astype(f32)
    bidx = jnp.arange(B)[:, None, None]

    def attend_block(blk):
        qb, qib, wib, qpos = blk
        s_idx = jax.nn.relu(jnp.einsum('bqhd,bsd->bqhs', qib.astype(f32), ki32))
        s_idx = jnp.einsum('bqhs,bqh->bqs', s_idx, wib)
        visible = (key_pos[None, :] <= qpos[:, None]) & (key_pos[None, :] >= 0)
        s_idx = jnp.where(visible[None], s_idx, NEG_INF)
        _, sel = lax.top_k(s_idx, topk)
        k_sel = k[bidx, sel].astype(f32)
        v_sel = v[bidx, sel].astype(f32)
        sel_pos = key_pos[sel]
        valid = (sel_pos <= qpos[None, :, None]) & (sel_pos >= 0)
        s = jnp.einsum('bqhd,bqkhd->bqhk', qb.astype(f32), k_sel) * (HEAD_DIM ** -0.5)
        s = jnp.where(valid[:, :, None, :], s, NEG_INF)
        return jnp.einsum('bqhk,bqkhd->bqhd', jax.nn.softmax(s, axis=-1), v_sel)

    out = lax.map(attend_block, (_to_blocks(q), _to_blocks(qi), _to_blocks(wi), key_pos.reshape(-1, CHUNK)))
    return _from_blocks(out)[:, N_PAD:].reshape(B, L, C_GRP).astype(p.dtype)


def _rglru(p, conv_w, conv_b, wa, ba, wx, bx, lam):
    f32 = jnp.float32
    xb, gate = _split(p, LRU_COLS)
    B, L, C = xb.shape
    xc = lax.conv_general_dilated(xb, conv_w[:, None, :].astype(xb.dtype), window_strides=(1,),
                                  padding=[(CONV_WIDTH - 1, 0)], dimension_numbers=('NWC', 'WIO', 'NWC'),
                                  feature_group_count=C) + conv_b
    xh = xc.reshape(B, L, H_GRP, HEAD_DIM)
    r = jax.nn.sigmoid((jnp.einsum('blhi,hij->blhj', xh, wa).reshape(B, L, C) + ba).astype(f32))
    i = jax.nn.sigmoid((jnp.einsum('blhi,hij->blhj', xh, wx).reshape(B, L, C) + bx).astype(f32))
    log_a = -LRU_C * r * jax.nn.softplus(-lam.astype(f32))
    a = jnp.exp(log_a)
    u = xc.astype(f32) * i * jnp.sqrt(-jnp.expm1(2.0 * log_a))

    def combine(e1, e2):
        a1, b1 = e1
        a2, b2 = e2
        return a1 * a2, a2 * b1 + b2

    _, h = lax.associative_scan(combine, (a, u), axis=1)
    return (h * jax.nn.gelu(gate.astype(f32))).astype(p.dtype)


def _hgrn2(p, lb, norm_g):
    f32 = jnp.float32
    q, fz, i, g = _split(p, HGRN_COLS)
    B, L, _ = q.shape
    lb = lb.astype(f32)
    z = fz.astype(f32)
    log_f = jnp.log(lb + (1.0 - lb) * jax.nn.sigmoid(z))
    k = (1.0 - lb) * jax.nn.sigmoid(-z)
    hs = lambda t: _pad_front(t.astype(f32).reshape(B, L, H_GRP, HEAD_DIM))
    qb, kb, vb, fb = (_to_blocks(hs(t)) for t in (jax.nn.silu(q.astype(f32)), k, i, log_f))
    tri = jnp.tril(jnp.ones((CHUNK, CHUNK), bool))[None, :, :, None, None]

    def chunk_step(S, inp):
        qc, kc, vc, lfc = inp
        bc = jnp.cumsum(lfc, axis=1)
        o = jnp.einsum('bthk,bhkv->bthv', qc * jnp.exp(bc), S)
        decay = jnp.exp(jnp.where(tri, bc[:, :, None] - bc[:, None, :], -jnp.inf))
        A = jnp.einsum('bthk,bshk,btshk->bths', qc, kc, decay)
        o = o + jnp.einsum('bths,bshv->bthv', A, vc)
        b_last = bc[:, -1]
        S = jnp.exp(b_last)[..., None] * S + jnp.einsum('bshk,bshv->bhkv', kc * jnp.exp(b_last[:, None] - bc), vc)
        return S, o

    S0 = jnp.zeros((B, H_GRP, HEAD_DIM, HEAD_DIM), f32)
    _, o = lax.scan(chunk_step, S0, (qb, kb, vb, fb))
    o = _from_blocks(o)[:, N_PAD:]
    o = o * lax.rsqrt(jnp.mean(o * o, axis=-1, keepdims=True) + RMS_EPS)
    o = o.reshape(B, L, C_GRP) * norm_g * jax.nn.silu(g.astype(f32))
    return o.astype(p.dtype)


def _moe(h, router_w, router_b, w1, b1, w2, b2):
    B, L, D = h.shape
    tok = h.reshape(B * L, D)
    logits = (tok @ router_w + router_b).astype(jnp.float32)
    top_val, top_idx = lax.top_k(logits, TOP_K)
    top_w = jax.nn.softmax(top_val, axis=-1)
    gates = jnp.sum(jax.nn.one_hot(top_idx, N_EXPERTS, dtype=jnp.float32) * top_w[..., None], axis=1)
    y = jnp.zeros((B * L, D), jnp.float32)
    for e in range(N_EXPERTS):
        u = (tok @ w1[e] + b1[e]).astype(jnp.float32)
        glu = jnp.minimum(u[:, :D_EXPERT], SWIGLU_LIMIT)
        lin = jnp.clip(u[:, D_EXPERT:], -SWIGLU_LIMIT, SWIGLU_LIMIT)
        act = glu * jax.nn.sigmoid(SWIGLU_ALPHA * glu) * (lin + 1.0)
        y = y + gates[:, e:e + 1] * (act.astype(h.dtype) @ w2[e] + b2[e]).astype(jnp.float32)
    return y.reshape(B, L, D).astype(h.dtype)


def setup_inputs(seed: int = 0) -> dict:
    key = jax.random.key(seed)
    ks = iter(jax.random.split(key, 48))
    f32 = jnp.float32

    def nrm(shape, scale):
        return scale * jax.random.normal(next(ks), shape, f32)

    def unif(shape, lo, hi):
        return jax.random.uniform(next(ks), shape, f32, lo, hi)

    Dp = DEPTH
    lru_a = unif((Dp, C_GRP), 0.9, 0.999) ** (1.0 / LRU_C)
    return {
        'x': nrm((BATCH, SEQ, D_MODEL), 1.0),
        'meta_tokens': nrm((N_META, D_MODEL), 1.0),
        'ln_in_g': 1.0 + nrm((D_MODEL,), 0.02),
        'ln_in_b': nrm((D_MODEL,), 0.02),
        'w_in': nrm((Dp, D_MODEL, N_IN), D_MODEL ** -0.5),
        'rwkv_mu': unif((Dp, GROUP_COLS[0]), 0.0, 1.0),
        'rwkv_w0': unif((Dp, C_GRP), -6.0, 0.0),
        'rwkv_w2': nrm((Dp, RWKV_DECAY_LORA, C_GRP), 0.1),
        'rwkv_a0': nrm((Dp, C_GRP), 0.5),
        'rwkv_a2': nrm((Dp, RWKV_AAA_LORA, C_GRP), 0.1),
        'rwkv_g2': nrm((Dp, RWKV_GATE_LORA, C_GRP), RWKV_GATE_LORA ** -0.5),
        'rwkv_k_k': 0.85 + nrm((Dp, C_GRP), 0.05),
        'rwkv_k_a': 1.0 + nrm((Dp, C_GRP), 0.05),
        'rwkv_r_k': nrm((Dp, H_GRP, HEAD_DIM), 0.1),
        'rwkv_ln_g': 1.0 + nrm((Dp, C_GRP), 0.02),
        'rwkv_ln_b': nrm((Dp, C_GRP), 0.02),
        'lru_conv_w': nrm((Dp, CONV_WIDTH, C_GRP), CONV_WIDTH ** -0.5),
        'lru_conv_b': nrm((Dp, C_GRP), 0.02),
        'lru_wa': nrm((Dp, H_GRP, HEAD_DIM, HEAD_DIM), HEAD_DIM ** -0.5),
        'lru_ba': nrm((Dp, C_GRP), 0.02),
        'lru_wx': nrm((Dp, H_GRP, HEAD_DIM, HEAD_DIM), HEAD_DIM ** -0.5),
        'lru_bx': nrm((Dp, C_GRP), 0.02),
        'lru_lambda': jnp.log(lru_a) - jnp.log1p(-lru_a),
        'hgrn_lower_bounds': 1.0 + nrm((Dp, C_GRP), 0.1),
        'hgrn_norm_g': 1.0 + nrm((Dp, C_GRP), 0.02),
        'w_out': nrm((Dp, D_MIX, D_MODEL), DEEPNORM_BETA * D_MIX ** -0.5),
        'ln_mix_g': 1.0 + nrm((Dp, D_MODEL), 0.02),
        'ln_mix_b': nrm((Dp, D_MODEL), 0.02),
        'router_w': nrm((Dp, D_MODEL, N_EXPERTS), D_MODEL ** -0.5),
        'router_b': nrm((Dp, N_EXPERTS), 0.01),
        'moe_w1': nrm((Dp, N_EXPERTS, D_MODEL, 2 * D_EXPERT), D_MODEL ** -0.5),
        'moe_b1': nrm((Dp, N_EXPERTS, 2 * D_EXPERT), 0.01),
        'moe_w2': nrm((Dp, N_EXPERTS, D_EXPERT, D_MODEL), DEEPNORM_BETA * D_EXPERT ** -0.5),
        'moe_b2': nrm((Dp, N_EXPERTS, D_MODEL), 0.01),
        'ln_ffn_g': 1.0 + nrm((Dp, D_MODEL), 0.02),
        'ln_ffn_b': nrm((Dp, D_MODEL), 0.02),
    }


def reference(x, meta_tokens, ln_in_g, ln_in_b, w_in, rwkv_mu, rwkv_w0, rwkv_w2, rwkv_a0, rwkv_a2,
              rwkv_g2, rwkv_k_k, rwkv_k_a, rwkv_r_k, rwkv_ln_g, rwkv_ln_b, lru_conv_w, lru_conv_b,
              lru_wa, lru_ba, lru_wx, lru_bx, lru_lambda, hgrn_lower_bounds, hgrn_norm_g, w_out,
              ln_mix_g, ln_mix_b, router_w, router_b, moe_w1, moe_b1, moe_w2, moe_b2,
              ln_ffn_g, ln_ffn_b):
    B, S, D = x.shape
    topk = min(INDEX_TOPK_MAX, S // 4)
    L = S + N_META
    pos = jnp.arange(L, dtype=jnp.int32)
    h = jnp.concatenate([jnp.broadcast_to(meta_tokens[None].astype(x.dtype), (B, N_META, D)), x], axis=1)
    h = _layernorm(h, ln_in_g, ln_in_b)
    s_lb = jax.nn.softmax(hgrn_lower_bounds.astype(jnp.float32), axis=0)
    lower_bounds = jnp.cumsum(s_lb, axis=0) - s_lb[0]
    for l in range(DEPTH):
        proj = h @ w_in[l]
        p_rwkv, p_dsa, p_lru, p_hgrn = _split(proj, GROUP_COLS)
        y_rwkv = _rwkv7(p_rwkv, rwkv_mu[l], rwkv_w0[l], rwkv_w2[l], rwkv_a0[l], rwkv_a2[l], rwkv_g2[l],
                        rwkv_k_k[l], rwkv_k_a[l], rwkv_r_k[l], rwkv_ln_g[l], rwkv_ln_b[l])
        y_dsa = _dsa_attention(p_dsa, pos, topk)
        y_lru = _rglru(p_lru, lru_conv_w[l], lru_conv_b[l], lru_wa[l], lru_ba[l], lru_wx[l], lru_bx[l],
                       lru_lambda[l])
        y_hgrn = _hgrn2(p_hgrn, lower_bounds[l], hgrn_norm_g[l])
        mix = jnp.concatenate([y_rwkv, y_dsa, y_lru, y_hgrn], axis=-1) @ w_out[l]
        h = _layernorm(DEEPNORM_ALPHA * h + mix, ln_mix_g[l], ln_mix_b[l])
        ffn = _moe(h, router_w[l], router_b[l], moe_w1[l], moe_b1[l], moe_w2[l], moe_b2[l])
        h = _layernorm(DEEPNORM_ALPHA * h + ffn, ln_ffn_g[l], ln_ffn_b[l])
    return h[:, N_META:]
```

```python
import functools
import math

import numpy as np
import jax
import jax.numpy as jnp
from jax import lax
from jax.experimental import pallas as pl
from jax.experimental.pallas import tpu as pltpu

F32 = jnp.float32
BF16 = jnp.bfloat16
HIGHEST = lax.Precision.HIGHEST

D_MODEL = 1024
N_META = 16
HEAD_DIM = 64
C_GRP = 256
H_GRP = 4
LORA = 64
IDX_HEADS = 8
IDX_DIM = 32
TOPK_MAX = 256
ROPE_THETA = 500000.0
LRU_C = 8.0
CONV_WIDTH = 4
LN_EPS = 1e-5
RMS_EPS = 1e-5
RWKV_LN_EPS = 64e-5
NEG_INF = -1e30
N_EXPERTS = 32
TOP_K = 4
SWIGLU_ALPHA = 1.702
SWIGLU_LIMIT = 7.0
DEPTH = 2
DEEPNORM_ALPHA = (2 * DEPTH) ** 0.25

SEQ_ALIGN = 256
RWKV_CHUNK = 64
SCAN_CHUNK = 128
DSA_TQ = 128
DSA_KB = 256
VMEM_LIMIT = 56 * 1024 * 1024

W_RWKV = 1024
W_DSA = 1152
W_LRU = 512
W_HGRN = 1024


def _cparams(sem):
    return pltpu.CompilerParams(dimension_semantics=sem, vmem_limit_bytes=VMEM_LIMIT)


def _mm(a, b):
    return jnp.dot(a.astype(BF16), b.astype(BF16), preferred_element_type=F32)


def _mm_nt(a, b):
    return lax.dot_general(a.astype(BF16), b.astype(BF16), (((1,), (1,)), ((), ())),
                           preferred_element_type=F32)


def _mm_tn(a, b):
    return lax.dot_general(a.astype(BF16), b.astype(BF16), (((0,), (0,)), ((), ())),
                           preferred_element_type=F32)


def _mmh(a, b):
    return jnp.dot(a, b, precision=HIGHEST, preferred_element_type=F32)


def _mmh_nt(a, b):
    return lax.dot_general(a, b, (((1,), (1,)), ((), ())), precision=HIGHEST,
                           preferred_element_type=F32)


def _sigmoid(x):
    return 1.0 / (1.0 + jnp.exp(-x))


def _softplus(x):
    return jnp.maximum(x, 0.0) + jnp.log1p(jnp.exp(-jnp.abs(x)))


def _head_masks():
    lane = lax.broadcasted_iota(jnp.int32, (1, C_GRP), 1) // HEAD_DIM
    return [(lane == h).astype(F32) for h in range(H_GRP)]


def _head_sum_matrix():
    r = lax.broadcasted_iota(jnp.int32, (C_GRP, C_GRP), 0) // HEAD_DIM
    c = lax.broadcasted_iota(jnp.int32, (C_GRP, C_GRP), 1) // HEAD_DIM
    return (r == c).astype(F32)


def _cumsum_rows(x):
    n = x.shape[0]
    row = lax.broadcasted_iota(jnp.int32, (n, 1), 0)
    s = 1
    while s < n:
        x = x + jnp.where(row >= s, pltpu.roll(x, s, axis=0), 0.0)
        s *= 2
    return x


def _layernorm_rows(x, g, b):
    mu = jnp.mean(x, axis=-1, keepdims=True)
    xc = x - mu
    var = jnp.mean(xc * xc, axis=-1, keepdims=True)
    return xc * lax.rsqrt(var + LN_EPS) * g + b


def _ln_kernel(x_ref, g_ref, b_ref, o_ref):
    o_ref[...] = _layernorm_rows(x_ref[...], g_ref[...], b_ref[...])


def _ln_call(x, g, b, tm):
    T, D = x.shape
    return pl.pallas_call(
        _ln_kernel,
        grid=(T // tm,),
        in_specs=[pl.BlockSpec((tm, D), lambda i: (i, 0)),
                  pl.BlockSpec((1, D), lambda i: (0, 0)),
                  pl.BlockSpec((1, D), lambda i: (0, 0))],
        out_specs=pl.BlockSpec((tm, D), lambda i: (i, 0)),
        out_shape=jax.ShapeDtypeStruct((T, D), F32),
        compiler_params=_cparams(("parallel",)),
        name="ln_in",
    )(x, g.reshape(1, D), b.reshape(1, D))


def _proj_kernel(h_ref, w_ref, o_rwkv, o_dsa, o_lru, o_hgrn):
    hb = h_ref[...].astype(BF16)
    off = 0
    for o, n in ((o_rwkv, W_RWKV), (o_dsa, W_DSA), (o_lru, W_LRU), (o_hgrn, W_HGRN)):
        o[...] = jnp.dot(hb, w_ref[:, off:off + n], preferred_element_type=F32)
        off += n


def _proj_call(h, w, tm):
    T, D = h.shape
    widths = (W_RWKV, W_DSA, W_LRU, W_HGRN)
    return pl.pallas_call(
        _proj_kernel,
        grid=(T // tm,),
        in_specs=[pl.BlockSpec((tm, D), lambda i: (i, 0)),
                  pl.BlockSpec((D, sum(widths)), lambda i: (0, 0))],
        out_specs=[pl.BlockSpec((tm, n), lambda i: (i, 0)) for n in widths],
        out_shape=[jax.ShapeDtypeStruct((T, n), F32) for n in widths],
        compiler_params=_cparams(("parallel",)),
        name="proj_in",
    )(h, w)


def _mix_kernel(h_ref, y0, y1, y2, y3, w_ref, g_ref, b_ref, o_ref):
    mix = _mm(y0[...], w_ref[0:C_GRP, :])
    mix += _mm(y1[...], w_ref[C_GRP:2 * C_GRP, :])
    mix += _mm(y2[...], w_ref[2 * C_GRP:3 * C_GRP, :])
    mix += _mm(y3[...], w_ref[3 * C_GRP:4 * C_GRP, :])
    o_ref[...] = _layernorm_rows(DEEPNORM_ALPHA * h_ref[...] + mix, g_ref[...], b_ref[...])


def _mix_call(h, ys, w, g, b, tm):
    T, D = h.shape
    return pl.pallas_call(
        _mix_kernel,
        grid=(T // tm,),
        in_specs=[pl.BlockSpec((tm, D), lambda i: (i, 0))]
        + [pl.BlockSpec((tm, C_GRP), lambda i: (i, 0))] * 4
        + [pl.BlockSpec((4 * C_GRP, D), lambda i: (0, 0)),
           pl.BlockSpec((1, D), lambda i: (0, 0)),
           pl.BlockSpec((1, D), lambda i: (0, 0))],
        out_specs=pl.BlockSpec((tm, D), lambda i: (i, 0)),
        out_shape=jax.ShapeDtypeStruct((T, D), F32),
        compiler_params=_cparams(("parallel",)),
        name="mix_out",
    )(h, *ys, w, g.reshape(1, D), b.reshape(1, D))


def _rwkv_kernel(p_ref, mu_ref, w0_ref, w2_ref, a0_ref, a2_ref, g2_ref, kk_ref, ka_ref, rk_ref,
                 lng_ref, lnb_ref, o_ref, prev_ref, s_ref):
    C = RWKV_CHUNK
    c = pl.program_id(1)

    @pl.when(c == 0)
    def _():
        prev_ref[...] = jnp.zeros_like(prev_ref)
        s_ref[...] = jnp.zeros_like(s_ref)

    p = p_ref[...]
    row = lax.broadcasted_iota(jnp.int32, (C, 1), 0)
    p_prev = jnp.where(row == 0, prev_ref[0:1, :], pltpu.roll(p, 1, axis=0))
    prev_ref[0:1, :] = p[C - 1:C, :]
    ps = p + (p_prev - p) * mu_ref[...]
    r = ps[:, 0:256]
    k = ps[:, 256:512]
    v = ps[:, 512:768]
    lora = ps[:, 768:1024]

    w = -_softplus(-(w0_ref[...] + _mm(jnp.tanh(lora), w2_ref[...]))) - 0.5
    logw = -jnp.exp(w)
    a = _sigmoid(a0_ref[...] + _mm(lora, a2_ref[...]))
    g = _mm(_sigmoid(lora), g2_ref[...])

    G = _head_sum_matrix()
    kk = k * kk_ref[...]
    kk = kk / jnp.maximum(jnp.sqrt(_mmh(kk * kk, G)), 1e-12)
    k_mod = k * (1.0 + (a - 1.0) * ka_ref[...])
    a_vec = -kk
    b_vec = kk * a

    cum = _cumsum_rows(logw)
    cum_last = cum[C - 1:C, :]
    e_inv = jnp.exp(-cum)
    g_end = jnp.exp(cum_last - cum)
    masks = _head_masks()

    def stack(x):
        return jnp.concatenate([x * m for m in masks], axis=0)

    Am = stack(a_vec * jnp.exp(cum - logw))
    Rm = stack(r * jnp.exp(cum))
    Bm = stack(b_vec * e_inv)
    Km = stack(k_mod * e_inv)
    Vs = stack(v)
    BKg = jnp.concatenate([stack(b_vec * g_end), stack(k_mod * g_end)], axis=0)

    ri = lax.broadcasted_iota(jnp.int32, (4 * C, 4 * C), 0)
    ci = lax.broadcasted_iota(jnp.int32, (4 * C, 4 * C), 1)
    same = (ri // C) == (ci // C)
    strict = same & (ci < ri)
    incl = same & (ci <= ri)

    N = jnp.where(strict, _mmh_nt(Am, Bm), 0.0)
    Aak = jnp.where(strict, _mmh_nt(Am, Km), 0.0)
    Arb = jnp.where(incl, _mmh_nt(Rm, Bm), 0.0)
    Ark = jnp.where(incl, _mmh_nt(Rm, Km), 0.0)

    P = jnp.where(ri == ci, 1.0, 0.0) + N
    M = N
    for _ in range(int(math.log2(C)) - 1):
        M = _mmh(M, M)
        P = P + _mmh(P, M)

    S = s_ref[...]
    U = _mmh(P, _mmh_nt(Am, S) + _mmh(Aak, Vs))
    Y = _mmh_nt(Rm, S) + _mmh(Arb, U) + _mmh(Ark, Vs)
    y = Y[0:C] + Y[C:2 * C] + Y[2 * C:3 * C] + Y[3 * C:4 * C]

    UV = jnp.concatenate([U, Vs], axis=0)
    upd = lax.dot_general(UV, BKg, (((0,), (0,)), ((), ())), precision=HIGHEST,
                          preferred_element_type=F32)
    s_ref[...] = S * jnp.exp(cum_last) + upd

    inv_n = 1.0 / HEAD_DIM
    y_mu = _mmh(y, G) * inv_n
    yc = y - y_mu
    y_var = _mmh(yc * yc, G) * inv_n
    yn = yc * lax.rsqrt(y_var + RWKV_LN_EPS) * lng_ref[...] + lnb_ref[...]
    bonus = _mmh(r * k_mod * rk_ref[...], G) * v
    o_ref[...] = (yn + bonus) * g


def _rwkv_call(p, prm, B, Lp):
    C = RWKV_CHUNK
    nC = Lp // C
    vec = lambda n: pl.BlockSpec((1, n), lambda b, c: (0, 0))
    mat = pl.BlockSpec((C_GRP, C_GRP), lambda b, c: (0, 0))
    return pl.pallas_call(
        _rwkv_kernel,
        grid=(B, nC),
        in_specs=[pl.BlockSpec((C, W_RWKV), lambda b, c: (b * nC + c, 0)),
                  vec(W_RWKV), vec(C_GRP), mat, vec(C_GRP), mat, mat,
                  vec(C_GRP), vec(C_GRP), vec(C_GRP), vec(C_GRP), vec(C_GRP)],
        out_specs=pl.BlockSpec((C, C_GRP), lambda b, c: (b * nC + c, 0)),
        out_shape=jax.ShapeDtypeStruct((B * Lp, C_GRP), F32),
        scratch_shapes=[pltpu.VMEM((8, W_RWKV), F32), pltpu.VMEM((C_GRP, C_GRP), F32)],
        compiler_params=_cparams(("parallel", "arbitrary")),
        name="rwkv7",
    )(p, *prm)


def _gelu_tanh(x):
    return 0.5 * x * (1.0 + jnp.tanh(math.sqrt(2.0 / math.pi) * (x + 0.044715 * x * x * x)))


def _lru_kernel(p_ref, cw_ref, cb_ref, wa_ref, ba_ref, wx_ref, bx_ref, lam_ref, o_ref,
                xbuf_ref, h_ref):
    C = SCAN_CHUNK
    c = pl.program_id(1)

    @pl.when(c == 0)
    def _():
        xbuf_ref[0:8, :] = jnp.zeros((8, C_GRP), F32)
        h_ref[...] = jnp.zeros_like(h_ref)

    xbuf_ref[8:8 + C, :] = p_ref[:, 0:C_GRP]
    gate = p_ref[:, C_GRP:2 * C_GRP]
    xc = cb_ref[...] + jnp.zeros((C, C_GRP), F32)
    for j in range(CONV_WIDTH):
        xc = xc + cw_ref[j:j + 1, :] * xbuf_ref[pl.ds(8 - (CONV_WIDTH - 1) + j, C), :]
    xbuf_ref[0:8, :] = xbuf_ref[C:C + 8, :]

    r = _sigmoid(_mm(xc, wa_ref[...]) + ba_ref[...])
    i = _sigmoid(_mm(xc, wx_ref[...]) + bx_ref[...])
    log_a = -LRU_C * r * _softplus(-lam_ref[...])
    a = jnp.exp(log_a)
    th = jnp.tanh(log_a)
    u = xc * i * jnp.sqrt(-2.0 * th / (1.0 - th))

    row = lax.broadcasted_iota(jnp.int32, (C, 1), 0)
    s = 1
    while s < C:
        keep = row >= s
        u = u + a * jnp.where(keep, pltpu.roll(u, s, axis=0), 0.0)
        a = a * jnp.where(keep, pltpu.roll(a, s, axis=0), 1.0)
        s *= 2
    h = u + a * h_ref[0:1, :]
    h_ref[0:1, :] = h[C - 1:C, :]
    o_ref[...] = h * _gelu_tanh(gate)


def _lru_call(p, prm, B, Lp):
    C = SCAN_CHUNK
    nC = Lp // C
    vec = pl.BlockSpec((1, C_GRP), lambda b, c: (0, 0))
    mat = pl.BlockSpec((C_GRP, C_GRP), lambda b, c: (0, 0))
    return pl.pallas_call(
        _lru_kernel,
        grid=(B, nC),
        in_specs=[pl.BlockSpec((C, W_LRU), lambda b, c: (b * nC + c, 0)),
                  pl.BlockSpec((CONV_WIDTH, C_GRP), lambda b, c: (0, 0)),
                  vec, mat, vec, mat, vec, vec],
        out_specs=pl.BlockSpec((C, C_GRP), lambda b, c: (b * nC + c, 0)),
        out_shape=jax.ShapeDtypeStruct((B * Lp, C_GRP), F32),
        scratch_shapes=[pltpu.VMEM((C + 8, C_GRP), F32), pltpu.VMEM((8, C_GRP), F32)],
        compiler_params=_cparams(("parallel", "arbitrary")),
        name="rglru",
    )(p, *prm)


def _hgrn_kernel(p_ref, lb_ref, ng_ref, o_ref, s_ref):
    C = SCAN_CHUNK
    c = pl.program_id(1)

    @pl.when(c == 0)
    def _():
        s_ref[...] = jnp.zeros_like(s_ref)

    q = p_ref[:, 0:256]
    z = p_ref[:, 256:512]
    v = p_ref[:, 512:768]
    g = p_ref[:, 768:1024]
    lb = lb_ref[...]
    lf = jnp.log(lb + (1.0 - lb) * _sigmoid(z))
    kx = (1.0 - lb) * _sigmoid(-z)
    qs = q * _sigmoid(q)
    bc = _cumsum_rows(lf)
    b_last = bc[C - 1:C, :]
    masks = _head_masks()
    G = _head_sum_matrix()
    S = s_ref[...]

    o = _mm_nt(qs * jnp.exp(bc), S)
    o = o + _mmh(qs * kx, G) * v

    ti = lax.broadcasted_iota(jnp.int32, (C, C), 0)
    si = lax.broadcasted_iota(jnp.int32, (C, C), 1)
    A = [jnp.zeros((C, C), F32) for _ in range(H_GRP)]
    bs = C
    while bs >= 2:
        hs = bs // 2
        sel = (si == (ti // bs) * bs + hs - 1).astype(F32)
        bcm = _mmh(sel, bc)
        qt = qs * jnp.exp(jnp.minimum(bc - bcm, 0.0))
        kt = kx * jnp.exp(jnp.minimum(bcm - bc, 0.0))
        lvl = ((ti // bs) == (si // bs)) & ((ti % bs) >= hs) & ((si % bs) < hs)
        for h in range(H_GRP):
            A[h] = A[h] + jnp.where(lvl, _mm_nt(qt * masks[h], kt), 0.0)
        bs = hs
    for h in range(H_GRP):
        o = o + _mm(A[h], v) * masks[h]

    upd = _mm_tn(v, kx * jnp.exp(b_last - bc))
    ri = lax.broadcasted_iota(jnp.int32, (C_GRP, C_GRP), 0) // HEAD_DIM
    ci = lax.broadcasted_iota(jnp.int32, (C_GRP, C_GRP), 1) // HEAD_DIM
    s_ref[...] = S * jnp.exp(b_last) + jnp.where(ri == ci, upd, 0.0)

    ms = _mmh(o * o, G) * (1.0 / HEAD_DIM)
    o_ref[...] = o * lax.rsqrt(ms + RMS_EPS) * ng_ref[...] * (g * _sigmoid(g))


def _hgrn_call(p, lb, ng, B, Lp):
    C = SCAN_CHUNK
    nC = Lp // C
    vec = pl.BlockSpec((1, C_GRP), lambda b, c: (0, 0))
    return pl.pallas_call(
        _hgrn_kernel,
        grid=(B, nC),
        in_specs=[pl.BlockSpec((C, W_HGRN), lambda b, c: (b * nC + c, 0)), vec, vec],
        out_specs=pl.BlockSpec((C, C_GRP), lambda b, c: (b * nC + c, 0)),
        out_shape=jax.ShapeDtypeStruct((B * Lp, C_GRP), F32),
        scratch_shapes=[pltpu.VMEM((C_GRP, C_GRP), F32)],
        compiler_params=_cparams(("parallel", "arbitrary")),
        name="hgrn2",
    )(p, lb, ng)


def _rope_tables(Lp, width, head, rd):
    half = rd // 2
    pos = np.arange(Lp, dtype=np.float32)
    inv_freq = jnp.asarray(ROPE_THETA, F32) ** (-jnp.arange(half, dtype=F32) / half)
    ang = jnp.asarray(pos)[:, None] * inv_freq[None, :]
    cos, sin = jnp.cos(ang), jnp.sin(ang)
    d = np.arange(width) % head
    fi = np.where(d < half, d, d - half) % half
    is1 = jnp.asarray(d < half)
    is2 = jnp.asarray((d >= half) & (d < rd))
    cosl, sinl = cos[:, fi], sin[:, fi]
    ct = jnp.where(is1 | is2, cosl, 1.0)
    s1 = jnp.where(is1, -sinl, 0.0)
    s2 = jnp.where(is2, sinl, 0.0)
    return ct, s1, s2


def _rope(x, ct, s1, s2, half):
    n = x.shape[1]
    return x * ct + pltpu.roll(x, n - half, axis=1) * s1 + pltpu.roll(x, half, axis=1) * s2


def _split3(x):
    x1 = x.astype(BF16)
    r1 = x - x1.astype(F32)
    x2 = r1.astype(BF16)
    x3 = (r1 - x2.astype(F32)).astype(BF16)
    return x1, x2, x3


def _dsa_prep_kernel(p_ref, cq_ref, s1q_ref, s2q_ref, ci_ref, s1i_ref, s2i_ref,
                     ck_ref, s1k_ref, s2k_ref, eq_ref, ek_ref,
                     q_out, k_out, v_out, qs_out, ks_out, kw_out):
    cq, s1q, s2q = cq_ref[...], s1q_ref[...], s2q_ref[...]
    q = _rope(p_ref[:, 0:256], cq, s1q, s2q, 8)
    q_out[...] = (q * (HEAD_DIM ** -0.5)).astype(BF16)
    k_out[...] = _rope(p_ref[:, 256:512], cq, s1q, s2q, 8).astype(BF16)
    v_out[...] = p_ref[:, 512:768].astype(BF16)
    qi = _rope(p_ref[:, 768:1024], ci_ref[...], s1i_ref[...], s2i_ref[...], 4)
    kw = _rope(p_ref[:, 1024:1152], ck_ref[...], s1k_ref[...], s2k_ref[...], 4)
    kw_out[...] = kw
    q1, q2, q3 = _split3(qi)
    qs = jnp.dot(q1, eq_ref[0], preferred_element_type=F32)
    qs += jnp.dot(q2, eq_ref[1], preferred_element_type=F32)
    qs += jnp.dot(q3, eq_ref[2], preferred_element_type=F32)
    qs_out[...] = qs.astype(BF16)
    k1, k2, k3 = _split3(kw)
    ks = jnp.dot(k1, ek_ref[0], preferred_element_type=F32)
    ks += jnp.dot(k2, ek_ref[1], preferred_element_type=F32)
    ks += jnp.dot(k3, ek_ref[2], preferred_element_type=F32)
    ks_out[...] = ks.astype(BF16)


def _split_expanders():
    q_slots = ((0, 1, 3), (2, 4), (5,))
    k_slots = ((0, 2, 5), (1, 4), (3,))
    eq = np.zeros((3, IDX_HEADS * IDX_DIM, IDX_HEADS * 256), np.float32)
    ek = np.zeros((3, 128, 256), np.float32)
    for part in range(3):
        for j in q_slots[part]:
            for h in range(IDX_HEADS):
                for d in range(IDX_DIM):
                    eq[part, h * IDX_DIM + d, h * 256 + j * IDX_DIM + d] = 1.0
        for j in k_slots[part]:
            for d in range(IDX_DIM):
                ek[part, d, j * IDX_DIM + d] = 1.0
    return jnp.asarray(eq, BF16), jnp.asarray(ek, BF16)


def _dsa_prep_call(p, B, Lp, tm):
    T = B * Lp
    nL = Lp // tm
    tq = _rope_tables(Lp, 256, HEAD_DIM, HEAD_DIM // 4)
    ti = _rope_tables(Lp, 256, IDX_DIM, IDX_DIM // 4)
    tk = tuple(jnp.concatenate([t, jnp.full((Lp, 96), fill, F32)], axis=1)
               for t, fill in zip(_rope_tables(Lp, 32, IDX_DIM, IDX_DIM // 4), (1.0, 0.0, 0.0)))
    eq, ek = _split_expanders()
    tab = lambda n: pl.BlockSpec((tm, n), lambda i: (i % nL, 0))
    row = lambda n: pl.BlockSpec((tm, n), lambda i: (i, 0))
    outs = [(256, BF16), (256, BF16), (256, BF16), (IDX_HEADS * 256, BF16), (256, BF16), (128, F32)]
    return pl.pallas_call(
        _dsa_prep_kernel,
        grid=(T // tm,),
        in_specs=[row(W_DSA)] + [tab(256)] * 6 + [tab(128)] * 3
        + [pl.BlockSpec(eq.shape, lambda i: (0, 0, 0)), pl.BlockSpec(ek.shape, lambda i: (0, 0, 0))],
        out_specs=[row(n) for n, _ in outs],
        out_shape=[jax.ShapeDtypeStruct((T, n), dt) for n, dt in outs],
        compiler_params=_cparams(("parallel",)),
        name="dsa_prep",
    )(p, *tq, *ti, *tk, eq, ek)


def _dsa_kernel(q_ref, qs_ref, kw_ref, k_ref, v_ref, kst_ref, o_ref,
                key_ref, acc_ref, m_ref, l_ref, *, topk):
    TQ, KB = DSA_TQ, DSA_KB
    i = pl.program_id(1)
    q0 = i * TQ
    nkb = (q0 + TQ + KB - 1) // KB
    qpos = q0 + lax.broadcasted_iota(jnp.int32, (TQ, 1), 0)
    col = lax.broadcasted_iota(jnp.int32, (1, KB), 1)
    int_min = jnp.int32(-2 ** 31)

    wi = kw_ref[:, IDX_DIM:IDX_DIM + IDX_HEADS] * (IDX_HEADS ** -0.5 * IDX_DIM ** -0.5)

    def idx_body(j, carry):
        ks = kst_ref[:, pl.ds(pl.multiple_of(j * KB, KB), KB)]
        acc = jnp.zeros((TQ, KB), F32)
        for h in range(IDX_HEADS):
            s = jnp.dot(qs_ref[:, h * 256:(h + 1) * 256], ks, preferred_element_type=F32)
            acc = acc + wi[:, h:h + 1] * jnp.maximum(s, 0.0)
        acc = jnp.where(j * KB + col <= qpos, acc, NEG_INF)
        bits = pltpu.bitcast(acc, jnp.int32)
        key_ref[:, pl.ds(pl.multiple_of(j * KB, KB), KB)] = bits ^ ((bits >> 31) & jnp.int32(0x7FFFFFFF))
        return carry

    lax.fori_loop(0, nkb, idx_body, 0)

    def count_ge(cand):
        def body(j, cnt):
            blk = key_ref[:, pl.ds(pl.multiple_of(j * KB, KB), KB)]
            hit = (blk >= cand).astype(jnp.int32)
            return cnt + hit[:, 0:128] + hit[:, 128:256]
        cnt = lax.fori_loop(0, nkb, body, jnp.zeros((TQ, 128), jnp.int32))
        return jnp.sum(cnt, axis=1, keepdims=True)

    def bit_body(b, res_u):
        cand_u = res_u | (jnp.int32(1) << (31 - b))
        cnt = count_ge(cand_u ^ int_min)
        return jnp.where(cnt >= topk, cand_u, res_u)

    res_u = lax.fori_loop(0, 32, bit_body, jnp.zeros((TQ, 1), jnp.int32))
    thr = res_u ^ int_min
    cnt_ge = count_ge(thr)
    cnt_gt = count_ge(thr + 1)
    n_take = topk - cnt_gt
    neg_key = jnp.int32(np.array(NEG_INF, np.float32).view(np.int32)) ^ jnp.int32(0x7FFFFFFF)
    need_rank = jnp.any(((cnt_ge - cnt_gt) > n_take) & (thr > neg_key))

    m_ref[...] = jnp.full(m_ref.shape, NEG_INF, F32)
    l_ref[...] = jnp.zeros(l_ref.shape, F32)
    acc_ref[...] = jnp.zeros(acc_ref.shape, F32)
    masks = _head_masks()
    qh = [(q_ref[...] * m.astype(BF16)) for m in masks]

    def attend(j, sel):
        kb = k_ref[pl.ds(pl.multiple_of(j * KB, KB), KB), :]
        vb = v_ref[pl.ds(pl.multiple_of(j * KB, KB), KB), :]
        for h in range(H_GRP):
            s = lax.dot_general(qh[h], kb, (((1,), (1,)), ((), ())), preferred_element_type=F32)
            s = jnp.where(sel, s, NEG_INF)
            m_old = m_ref[h]
            m_new = jnp.maximum(m_old, jnp.max(s, axis=1, keepdims=True))
            alpha = jnp.exp(m_old - m_new)
            pexp = jnp.where(sel, jnp.exp(s - m_new), 0.0)
            l_ref[h] = l_ref[h] * alpha + jnp.sum(pexp, axis=1, keepdims=True)
            acc_ref[h] = acc_ref[h] * alpha + jnp.dot(pexp.astype(BF16), vb, preferred_element_type=F32)
            m_ref[h] = m_new

    def fast_body(j, carry):
        blk = key_ref[:, pl.ds(pl.multiple_of(j * KB, KB), KB)]
        attend(j, (blk >= thr) & (j * KB + col <= qpos))
        return carry

    def rank_body(j, seen):
        blk = key_ref[:, pl.ds(pl.multiple_of(j * KB, KB), KB)]
        eq = blk == thr
        ri = lax.broadcasted_iota(jnp.int32, (KB, KB), 0)
        ci = lax.broadcasted_iota(jnp.int32, (KB, KB), 1)
        upper = (ri <= ci).astype(BF16)
        rank = seen + jnp.dot(eq.astype(BF16), upper, preferred_element_type=F32)
        sel = ((blk > thr) | (eq & (rank <= n_take.astype(F32)))) & (j * KB + col <= qpos)
        attend(j, sel)
        return seen + jnp.sum(eq.astype(F32), axis=1, keepdims=True)

    @pl.when(jnp.logical_not(need_rank))
    def _():
        lax.fori_loop(0, nkb, fast_body, 0)

    @pl.when(need_rank)
    def _():
        lax.fori_loop(0, nkb, rank_body, jnp.zeros((TQ, 1), F32))

    out = jnp.zeros((TQ, C_GRP), F32)
    for h in range(H_GRP):
        out = out + acc_ref[h] * (masks[h] / l_ref[h])
    o_ref[...] = out


def _dsa_call(q, qs, kw, k3, v3, kst, B, Lp, topk):
    TQ = DSA_TQ
    nQ = Lp // TQ
    rowq = lambda n: pl.BlockSpec((TQ, n), lambda b, i: (b * nQ + i, 0))
    return pl.pallas_call(
        functools.partial(_dsa_kernel, topk=topk),
        grid=(B, nQ),
        in_specs=[rowq(256), rowq(IDX_HEADS * 256), rowq(128),
                  pl.BlockSpec((None, Lp, 256), lambda b, i: (b, 0, 0)),
                  pl.BlockSpec((None, Lp, 256), lambda b, i: (b, 0, 0)),
                  pl.BlockSpec((None, 256, Lp), lambda b, i: (b, 0, 0))],
        out_specs=rowq(C_GRP),
        out_shape=jax.ShapeDtypeStruct((B * Lp, C_GRP), F32),
        scratch_shapes=[pltpu.VMEM((TQ, Lp), jnp.int32),
                        pltpu.VMEM((H_GRP, TQ, C_GRP), F32),
                        pltpu.VMEM((H_GRP, TQ, 1), F32),
                        pltpu.VMEM((H_GRP, TQ, 1), F32)],
        compiler_params=_cparams(("parallel", "arbitrary")),
        name="dsa_attn",
    )(q, qs, kw, k3, v3, kst)


def _moe_kernel(x_ref, rw_ref, rb_ref, w1_ref, b1_ref, w2_ref, b2_ref, g_ref, b_ref, o_ref,
                gate_ref, acc_ref):
    e = pl.program_id(1)
    tm = x_ref.shape[0]
    x = x_ref[...]
    lane = lax.broadcasted_iota(jnp.int32, (tm, N_EXPERTS), 1)

    @pl.when(e == 0)
    def _():
        logits = _mmh(x, rw_ref[...]) + rb_ref[...]
        work = logits
        vals, hots = [], []
        for _ in range(TOP_K):
            mx = jnp.max(work, axis=1, keepdims=True)
            first = jnp.min(jnp.where(work == mx, lane, N_EXPERTS), axis=1, keepdims=True)
            hot = lane == first
            vals.append(mx)
            hots.append(hot)
            work = jnp.where(hot, -jnp.inf, work)
        ex = [jnp.exp(vv - vals[0]) for vv in vals]
        den = ex[0] + ex[1] + ex[2] + ex[3]
        gates = jnp.zeros((tm, N_EXPERTS), F32)
        for hot, w in zip(hots, ex):
            gates = gates + jnp.where(hot, w / den, 0.0)
        gate_ref[...] = gates
        acc_ref[...] = jnp.zeros_like(acc_ref)

    xb = x.astype(BF16)
    gate_e = jnp.sum(jnp.where(lane == e, gate_ref[...], 0.0), axis=1, keepdims=True)
    DE = D_MODEL
    y = jnp.zeros((tm, D_MODEL), F32) + b2_ref[0]
    nj = 4
    wj = DE // nj
    for j in range(nj):
        glu = jnp.dot(xb, w1_ref[0, :, j * wj:(j + 1) * wj], preferred_element_type=F32)
        glu = jnp.minimum(glu + b1_ref[0, :, j * wj:(j + 1) * wj], SWIGLU_LIMIT)
        lin = jnp.dot(xb, w1_ref[0, :, DE + j * wj:DE + (j + 1) * wj], preferred_element_type=F32)
        lin = jnp.clip(lin + b1_ref[0, :, DE + j * wj:DE + (j + 1) * wj], -SWIGLU_LIMIT, SWIGLU_LIMIT)
        act = glu * _sigmoid(SWIGLU_ALPHA * glu) * (lin + 1.0)
        y = y + jnp.dot(act.astype(BF16), w2_ref[0, j * wj:(j + 1) * wj, :], preferred_element_type=F32)
    acc_ref[...] += gate_e * y

    @pl.when(e == N_EXPERTS - 1)
    def _():
        o_ref[...] = _layernorm_rows(DEEPNORM_ALPHA * x + acc_ref[...], g_ref[...], b_ref[...])


def _moe_call(h, rw, rb, w1, b1, w2, b2, g, b, tm):
    T, D = h.shape
    E = N_EXPERTS
    return pl.pallas_call(
        _moe_kernel,
        grid=(T // tm, E),
        in_specs=[pl.BlockSpec((tm, D), lambda i, e: (i, 0)),
                  pl.BlockSpec((D, E), lambda i, e: (0, 0)),
                  pl.BlockSpec((1, E), lambda i, e: (0, 0)),
                  pl.BlockSpec((1, D, 2 * D), lambda i, e: (e, 0, 0)),
                  pl.BlockSpec((1, 1, 2 * D), lambda i, e: (e, 0, 0)),
                  pl.BlockSpec((1, D, D), lambda i, e: (e, 0, 0)),
                  pl.BlockSpec((1, 1, D), lambda i, e: (e, 0, 0)),
                  pl.BlockSpec((1, D), lambda i, e: (0, 0)),
                  pl.BlockSpec((1, D), lambda i, e: (0, 0))],
        out_specs=pl.BlockSpec((tm, D), lambda i, e: (i, 0)),
        out_shape=jax.ShapeDtypeStruct((T, D), F32),
        scratch_shapes=[pltpu.VMEM((tm, E), F32), pltpu.VMEM((tm, D), F32)],
        compiler_params=_cparams(("parallel", "arbitrary")),
        name="moe",
    )(h, rw, rb.reshape(1, E), w1, b1.reshape(E, 1, 2 * D), w2, b2.reshape(E, 1, D),
      g.reshape(1, D), b.reshape(1, D))


def _regroup_w_in(w):
    D = w.shape[0]
    z = lambda n: jnp.zeros((D, n), w.dtype)
    o = 0
    r, wd, k, v, ad, gd = (w[:, a:b] for a, b in _bounds(o, (256, 64, 256, 256, 64, 64)))
    o += 960
    dsa = w[:, o:o + 1064]
    o += 1064
    lru = w[:, o:o + 512]
    o += 512
    hg = w[:, o:o + 1024]
    return jnp.concatenate([r, k, v, wd, ad, gd, z(64), dsa, z(W_DSA - 1064), lru, hg], axis=1).astype(BF16)


def _bounds(start, sizes):
    out = []
    for s in sizes:
        out.append((start, start + s))
        start += s
    return out


def _regroup_mu(mu):
    r, wd, k, v, ad, gd = (mu[a:b] for a, b in _bounds(0, (256, 64, 256, 256, 64, 64)))
    return jnp.concatenate([r, k, v, wd, ad, gd, jnp.zeros((64,), mu.dtype)]).reshape(1, W_RWKV)


def _lora_pad(w, slot):
    return jnp.zeros((C_GRP, C_GRP), w.dtype).at[slot * LORA:(slot + 1) * LORA].set(w).astype(BF16)


def _block_diag(w):
    out = jnp.zeros((C_GRP, C_GRP), w.dtype)
    for h in range(H_GRP):
        out = out.at[h * HEAD_DIM:(h + 1) * HEAD_DIM, h * HEAD_DIM:(h + 1) * HEAD_DIM].set(w[h])
    return out.astype(BF16)


def _pick_tile(T, pref):
    for tm in (pref, 512, 256, 128, 64, 32, 16, 8):
        if tm <= pref and T % tm == 0:
            return tm
    raise ValueError(f"no row tile for {T}")


def kernel(x, meta_tokens, ln_in_g, ln_in_b, w_in, rwkv_mu, rwkv_w0, rwkv_w2, rwkv_a0, rwkv_a2,
           rwkv_g2, rwkv_k_k, rwkv_k_a, rwkv_r_k, rwkv_ln_g, rwkv_ln_b, lru_conv_w, lru_conv_b,
           lru_wa, lru_ba, lru_wx, lru_bx, lru_lambda, hgrn_lower_bounds, hgrn_norm_g, w_out,
           ln_mix_g, ln_mix_b, router_w, router_b, moe_w1, moe_b1, moe_w2, moe_b2,
           ln_ffn_g, ln_ffn_b):
    B, S, D = x.shape
    assert D == D_MODEL
    depth = w_in.shape[0]
    topk = min(TOPK_MAX, S // 4)
    L = S + N_META
    Lp = -(-L // SEQ_ALIGN) * SEQ_ALIGN
    T = B * Lp
    tm = _pick_tile(T, 512)
    v1 = lambda t: t.reshape(1, -1)

    h = jnp.concatenate([jnp.broadcast_to(meta_tokens[None].astype(x.dtype), (B, N_META, D)), x,
                         jnp.zeros((B, Lp - L, D), x.dtype)], axis=1).reshape(T, D)
    h = _ln_call(h, ln_in_g, ln_in_b, tm)

    s_lb = jax.nn.softmax(hgrn_lower_bounds.astype(F32), axis=0)
    lower_bounds = jnp.cumsum(s_lb, axis=0) - s_lb[0]

    for l in range(depth):
        p_rwkv, p_dsa, p_lru, p_hgrn = _proj_call(h, _regroup_w_in(w_in[l]), tm)

        y_rwkv = _rwkv_call(p_rwkv, (
            _regroup_mu(rwkv_mu[l]), v1(rwkv_w0[l]), _lora_pad(rwkv_w2[l], 0), v1(rwkv_a0[l]),
            _lora_pad(rwkv_a2[l], 1), _lora_pad(rwkv_g2[l], 2), v1(rwkv_k_k[l]), v1(rwkv_k_a[l]),
            v1(rwkv_r_k[l]), v1(rwkv_ln_g[l]), v1(rwkv_ln_b[l])), B, Lp)

        q, k, v, qs, ks, kw = _dsa_prep_call(p_dsa, B, Lp, _pick_tile(Lp, 512))
        kst = jnp.swapaxes(ks.reshape(B, Lp, 256), 1, 2)
        y_dsa = _dsa_call(q, qs, kw, k.reshape(B, Lp, 256), v.reshape(B, Lp, 256), kst, B, Lp, topk)

        y_lru = _lru_call(p_lru, (lru_conv_w[l], v1(lru_conv_b[l]), _block_diag(lru_wa[l]),
                                  v1(lru_ba[l]), _block_diag(lru_wx[l]), v1(lru_bx[l]),
                                  v1(lru_lambda[l])), B, Lp)

        y_hgrn = _hgrn_call(p_hgrn, v1(lower_bounds[l]), v1(hgrn_norm_g[l]), B, Lp)

        h = _mix_call(h, (y_rwkv, y_dsa, y_lru, y_hgrn), w_out[l].astype(BF16),
                      ln_mix_g[l], ln_mix_b[l], tm)
        h = _moe_call(h, router_w[l], router_b[l], moe_w1[l].astype(BF16), moe_b1[l],
                      moe_w2[l].astype(BF16), moe_b2[l], ln_ffn_g[l], ln_ffn_b[l], tm)

    return h.reshape(B, Lp, D)[:, N_META:L]
```

```python
import functools
import math

import numpy as np
import jax
import jax.numpy as jnp
from jax import lax
from jax.experimental import pallas as pl
from jax.experimental.pallas import tpu as pltpu

F32 = jnp.float32
BF16 = jnp.bfloat16
HIGHEST = lax.Precision.HIGHEST

D_MODEL = 1024
N_META = 16
HEAD_DIM = 64
C_GRP = 256
H_GRP = 4
LORA = 64
IDX_HEADS = 8
IDX_DIM = 32
TOPK_MAX = 256
ROPE_THETA = 500000.0
LRU_C = 8.0
CONV_WIDTH = 4
LN_EPS = 1e-5
RMS_EPS = 1e-5
RWKV_LN_EPS = 64e-5
NEG_INF = -1e30
N_EXPERTS = 32
TOP_K = 4
SWIGLU_ALPHA = 1.702
SWIGLU_LIMIT = 7.0
DEPTH = 2
DEEPNORM_ALPHA = (2 * DEPTH) ** 0.25

SEQ_ALIGN = 256
RWKV_CHUNK = 64
SCAN_CHUNK = 128
DSA_TQ = 256
DSA_KB = 256
VMEM_LIMIT = 56 * 1024 * 1024

W_RWKV = 1024
W_DSA = 1152
W_LRU = 512
W_HGRN = 1024


def _cparams(sem):
    return pltpu.CompilerParams(dimension_semantics=sem, vmem_limit_bytes=VMEM_LIMIT)


def _mm(a, b):
    return jnp.dot(a.astype(BF16), b.astype(BF16), preferred_element_type=F32)


def _mm_nt(a, b):
    return lax.dot_general(a.astype(BF16), b.astype(BF16), (((1,), (1,)), ((), ())),
                           preferred_element_type=F32)


def _mm_tn(a, b):
    return lax.dot_general(a.astype(BF16), b.astype(BF16), (((0,), (0,)), ((), ())),
                           preferred_element_type=F32)


def _mmh(a, b):
    return jnp.dot(a, b, precision=HIGHEST, preferred_element_type=F32)


def _split2(x):
    hi = x.astype(BF16)
    return hi, (x - hi.astype(F32)).astype(BF16)


def _split3(x):
    x1 = x.astype(BF16)
    r1 = x - x1.astype(F32)
    x2 = r1.astype(BF16)
    x3 = (r1 - x2.astype(F32)).astype(BF16)
    return x1, x2, x3


def _mm3(a, b):
    ah, al = _split2(a)
    bh, bl = _split2(b)
    d = lambda x, y: jnp.dot(x, y, preferred_element_type=F32)
    return d(ah, bh) + (d(ah, bl) + d(al, bh))


def _mm_ones(a, g):
    hi, lo = _split2(a)
    gb = g.astype(BF16)
    return (jnp.dot(hi, gb, preferred_element_type=F32) + jnp.dot(lo, gb, preferred_element_type=F32))


def _select_rows(sel, x):
    x1, x2, x3 = _split3(x)
    sb = sel.astype(BF16)
    return (jnp.dot(sb, x1, preferred_element_type=F32) + jnp.dot(sb, x2, preferred_element_type=F32)
            + jnp.dot(sb, x3, preferred_element_type=F32))


def _sigmoid(x):
    return 1.0 / (1.0 + jnp.exp(-x))


def _softplus(x):
    return jnp.maximum(x, 0.0) + jnp.log1p(jnp.exp(-jnp.abs(x)))


def _head_masks():
    lane = lax.broadcasted_iota(jnp.int32, (1, C_GRP), 1) // HEAD_DIM
    return [(lane == h).astype(F32) for h in range(H_GRP)]


def _head_sum_matrix():
    r = lax.broadcasted_iota(jnp.int32, (C_GRP, C_GRP), 0) // HEAD_DIM
    c = lax.broadcasted_iota(jnp.int32, (C_GRP, C_GRP), 1) // HEAD_DIM
    return (r == c).astype(F32)


def _cumsum_rows(x):
    n = x.shape[0]
    row = lax.broadcasted_iota(jnp.int32, (n, 1), 0)
    s = 1
    while s < n:
        x = x + jnp.where(row >= s, pltpu.roll(x, s, axis=0), 0.0)
        s *= 2
    return x


def _layernorm_rows(x, g, b):
    mu = jnp.mean(x, axis=-1, keepdims=True)
    xc = x - mu
    var = jnp.mean(xc * xc, axis=-1, keepdims=True)
    return xc * lax.rsqrt(var + LN_EPS) * g + b


def _ln_kernel(x_ref, g_ref, b_ref, o_ref):
    o_ref[...] = _layernorm_rows(x_ref[...], g_ref[...], b_ref[...])


def _ln_call(x, g, b, tm):
    T, D = x.shape
    return pl.pallas_call(
        _ln_kernel,
        grid=(T // tm,),
        in_specs=[pl.BlockSpec((tm, D), lambda i: (i, 0)),
                  pl.BlockSpec((1, D), lambda i: (0, 0)),
                  pl.BlockSpec((1, D), lambda i: (0, 0))],
        out_specs=pl.BlockSpec((tm, D), lambda i: (i, 0)),
        out_shape=jax.ShapeDtypeStruct((T, D), F32),
        compiler_params=_cparams(("parallel",)),
        name="ln_in",
    )(x, g.reshape(1, D), b.reshape(1, D))


def _proj_kernel(h_ref, w_ref, o_rwkv, o_dsa, o_lru, o_hgrn):
    hb = h_ref[...].astype(BF16)
    off = 0
    for o, n in ((o_rwkv, W_RWKV), (o_dsa, W_DSA), (o_lru, W_LRU), (o_hgrn, W_HGRN)):
        o[...] = jnp.dot(hb, w_ref[:, off:off + n], preferred_element_type=F32)
        off += n


def _proj_call(h, w, tm):
    T, D = h.shape
    widths = (W_RWKV, W_DSA, W_LRU, W_HGRN)
    return pl.pallas_call(
        _proj_kernel,
        grid=(T // tm,),
        in_specs=[pl.BlockSpec((tm, D), lambda i: (i, 0)),
                  pl.BlockSpec((D, sum(widths)), lambda i: (0, 0))],
        out_specs=[pl.BlockSpec((tm, n), lambda i: (i, 0)) for n in widths],
        out_shape=[jax.ShapeDtypeStruct((T, n), F32) for n in widths],
        compiler_params=_cparams(("parallel",)),
        name="proj_in",
    )(h, w)


def _mix_kernel(h_ref, y0, y1, y2, y3, w_ref, g_ref, b_ref, o_ref):
    mix = _mm(y0[...], w_ref[0:C_GRP, :])
    mix += _mm(y1[...], w_ref[C_GRP:2 * C_GRP, :])
    mix += _mm(y2[...], w_ref[2 * C_GRP:3 * C_GRP, :])
    mix += _mm(y3[...], w_ref[3 * C_GRP:4 * C_GRP, :])
    o_ref[...] = _layernorm_rows(DEEPNORM_ALPHA * h_ref[...] + mix, g_ref[...], b_ref[...])


def _mix_call(h, ys, w, g, b, tm):
    T, D = h.shape
    return pl.pallas_call(
        _mix_kernel,
        grid=(T // tm,),
        in_specs=[pl.BlockSpec((tm, D), lambda i: (i, 0))]
        + [pl.BlockSpec((tm, C_GRP), lambda i: (i, 0))] * 4
        + [pl.BlockSpec((4 * C_GRP, D), lambda i: (0, 0)),
           pl.BlockSpec((1, D), lambda i: (0, 0)),
           pl.BlockSpec((1, D), lambda i: (0, 0))],
        out_specs=pl.BlockSpec((tm, D), lambda i: (i, 0)),
        out_shape=jax.ShapeDtypeStruct((T, D), F32),
        compiler_params=_cparams(("parallel",)),
        name="mix_out",
    )(h, *ys, w, g.reshape(1, D), b.reshape(1, D))


def _rwkv_kernel(p_ref, mu_ref, w0_ref, w2_ref, a0_ref, a2_ref, g2_ref, kk_ref, ka_ref, rk_ref,
                 lng_ref, lnb_ref, o_ref, prev_ref, s_ref):
    C = RWKV_CHUNK
    c = pl.program_id(1)

    @pl.when(c == 0)
    def _():
        prev_ref[...] = jnp.zeros_like(prev_ref)
        s_ref[...] = jnp.zeros_like(s_ref)

    p = p_ref[...]
    row = lax.broadcasted_iota(jnp.int32, (C, 1), 0)
    p_prev = jnp.where(row == 0, prev_ref[0:1, :], pltpu.roll(p, 1, axis=0))
    prev_ref[0:1, :] = p[C - 1:C, :]
    ps = p + (p_prev - p) * mu_ref[...]
    r = ps[:, 0:256]
    k = ps[:, 256:512]
    v = ps[:, 512:768]
    lora = ps[:, 768:1024]

    w = -_softplus(-(w0_ref[...] + _mm(jnp.tanh(lora), w2_ref[...]))) - 0.5
    logw = -jnp.exp(w)
    a = _sigmoid(a0_ref[...] + _mm(lora, a2_ref[...]))
    g = _mm(_sigmoid(lora), g2_ref[...])

    G = _head_sum_matrix()
    kk = k * kk_ref[...]
    kk = kk / jnp.maximum(jnp.sqrt(_mm_ones(kk * kk, G)), 1e-12)
    k_mod = k * (1.0 + (a - 1.0) * ka_ref[...])
    a_vec = -kk
    b_vec = kk * a

    cum = _cumsum_rows(logw)
    cum_last = cum[C - 1:C, :]
    e_inv = jnp.exp(-cum)
    g_end = jnp.exp(cum_last - cum)
    masks = _head_masks()

    def stack(x):
        return jnp.concatenate([x * m for m in masks], axis=0)

    Am = stack(a_vec * jnp.exp(cum - logw))
    Rm = stack(r * jnp.exp(cum))
    Bm = stack(b_vec * e_inv)
    Km = stack(k_mod * e_inv)
    Vs = stack(v)
    BKg = jnp.concatenate([stack(b_vec * g_end), stack(k_mod * g_end)], axis=0)

    ri = lax.broadcasted_iota(jnp.int32, (4 * C, 4 * C), 0)
    ci = lax.broadcasted_iota(jnp.int32, (4 * C, 4 * C), 1)
    same = (ri // C) == (ci // C)
    strict = same & (ci < ri)
    incl = same & (ci <= ri)

    N = jnp.where(strict, _mm_nt(Am, Bm), 0.0)
    Aak = jnp.where(strict, _mm_nt(Am, Km), 0.0)
    Arb = jnp.where(incl, _mm_nt(Rm, Bm), 0.0)
    Ark = jnp.where(incl, _mm_nt(Rm, Km), 0.0)

    base = 8
    Nb = jnp.where((ri // base) == (ci // base), N, 0.0)
    P = jnp.where(ri == ci, 1.0, 0.0) + Nb
    M = Nb
    for _ in range(int(math.log2(base)) - 1):
        M = _mm3(M, M)
        P = P + _mm3(P, M)
    s = base
    while s < C:
        lower_left = ((ri // (2 * s)) == (ci // (2 * s))) & ((ri // s) % 2 == 1) & ((ci // s) % 2 == 0)
        P = P + _mm(P, _mm(jnp.where(lower_left, N, 0.0), P))
        s *= 2

    S = s_ref[...]
    U = _mm(P, _mm_nt(Am, S) + _mm(Aak, Vs))
    Y = _mm_nt(Rm, S) + _mm(Arb, U) + _mm(Ark, Vs)
    y = Y[0:C] + Y[C:2 * C] + Y[2 * C:3 * C] + Y[3 * C:4 * C]

    UV = jnp.concatenate([U, Vs], axis=0)
    s_ref[...] = S * jnp.exp(cum_last) + _mm_tn(UV, BKg)

    inv_n = 1.0 / HEAD_DIM
    y_mu = _mm_ones(y, G) * inv_n
    yc = y - y_mu
    y_var = _mm_ones(yc * yc, G) * inv_n
    yn = yc * lax.rsqrt(y_var + RWKV_LN_EPS) * lng_ref[...] + lnb_ref[...]
    bonus = _mm_ones(r * k_mod * rk_ref[...], G) * v
    o_ref[...] = (yn + bonus) * g


def _rwkv_call(p, prm, B, Lp):
    C = RWKV_CHUNK
    nC = Lp // C
    vec = lambda n: pl.BlockSpec((1, n), lambda b, c: (0, 0))
    mat = pl.BlockSpec((C_GRP, C_GRP), lambda b, c: (0, 0))
    return pl.pallas_call(
        _rwkv_kernel,
        grid=(B, nC),
        in_specs=[pl.BlockSpec((C, W_RWKV), lambda b, c: (b * nC + c, 0)),
                  vec(W_RWKV), vec(C_GRP), mat, vec(C_GRP), mat, mat,
                  vec(C_GRP), vec(C_GRP), vec(C_GRP), vec(C_GRP), vec(C_GRP)],
        out_specs=pl.BlockSpec((C, C_GRP), lambda b, c: (b * nC + c, 0)),
        out_shape=jax.ShapeDtypeStruct((B * Lp, C_GRP), F32),
        scratch_shapes=[pltpu.VMEM((8, W_RWKV), F32), pltpu.VMEM((C_GRP, C_GRP), F32)],
        compiler_params=_cparams(("parallel", "arbitrary")),
        name="rwkv7",
    )(p, *prm)


def _gelu_tanh(x):
    return 0.5 * x * (1.0 + jnp.tanh(math.sqrt(2.0 / math.pi) * (x + 0.044715 * x * x * x)))


def _lru_kernel(p_ref, cw_ref, cb_ref, wa_ref, ba_ref, wx_ref, bx_ref, lam_ref, o_ref,
                xbuf_ref, h_ref):
    C = SCAN_CHUNK
    c = pl.program_id(1)

    @pl.when(c == 0)
    def _():
        xbuf_ref[0:8, :] = jnp.zeros((8, C_GRP), F32)
        h_ref[...] = jnp.zeros_like(h_ref)

    xbuf_ref[8:8 + C, :] = p_ref[:, 0:C_GRP]
    gate = p_ref[:, C_GRP:2 * C_GRP]
    xc = cb_ref[...] + jnp.zeros((C, C_GRP), F32)
    for j in range(CONV_WIDTH):
        xc = xc + cw_ref[j:j + 1, :] * xbuf_ref[pl.ds(8 - (CONV_WIDTH - 1) + j, C), :]
    xbuf_ref[0:8, :] = xbuf_ref[C:C + 8, :]

    r = _sigmoid(_mm(xc, wa_ref[...]) + ba_ref[...])
    i = _sigmoid(_mm(xc, wx_ref[...]) + bx_ref[...])
    log_a = -LRU_C * r * _softplus(-lam_ref[...])
    a = jnp.exp(log_a)
    th = jnp.tanh(log_a)
    u = xc * i * jnp.sqrt(-2.0 * th / (1.0 - th))

    row = lax.broadcasted_iota(jnp.int32, (C, 1), 0)
    s = 1
    while s < C:
        keep = row >= s
        u = u + a * jnp.where(keep, pltpu.roll(u, s, axis=0), 0.0)
        a = a * jnp.where(keep, pltpu.roll(a, s, axis=0), 1.0)
        s *= 2
    h = u + a * h_ref[0:1, :]
    h_ref[0:1, :] = h[C - 1:C, :]
    o_ref[...] = h * _gelu_tanh(gate)


def _lru_call(p, prm, B, Lp):
    C = SCAN_CHUNK
    nC = Lp // C
    vec = pl.BlockSpec((1, C_GRP), lambda b, c: (0, 0))
    mat = pl.BlockSpec((C_GRP, C_GRP), lambda b, c: (0, 0))
    return pl.pallas_call(
        _lru_kernel,
        grid=(B, nC),
        in_specs=[pl.BlockSpec((C, W_LRU), lambda b, c: (b * nC + c, 0)),
                  pl.BlockSpec((CONV_WIDTH, C_GRP), lambda b, c: (0, 0)),
                  vec, mat, vec, mat, vec, vec],
        out_specs=pl.BlockSpec((C, C_GRP), lambda b, c: (b * nC + c, 0)),
        out_shape=jax.ShapeDtypeStruct((B * Lp, C_GRP), F32),
        scratch_shapes=[pltpu.VMEM((C + 8, C_GRP), F32), pltpu.VMEM((8, C_GRP), F32)],
        compiler_params=_cparams(("parallel", "arbitrary")),
        name="rglru",
    )(p, *prm)


def _hgrn_kernel(p_ref, lb_ref, ng_ref, o_ref, s_ref):
    C = SCAN_CHUNK
    c = pl.program_id(1)

    @pl.when(c == 0)
    def _():
        s_ref[...] = jnp.zeros_like(s_ref)

    q = p_ref[:, 0:256]
    z = p_ref[:, 256:512]
    v = p_ref[:, 512:768]
    g = p_ref[:, 768:1024]
    lb = lb_ref[...]
    lf = jnp.log(lb + (1.0 - lb) * _sigmoid(z))
    kx = (1.0 - lb) * _sigmoid(-z)
    qs = q * _sigmoid(q)
    bc = _cumsum_rows(lf)
    b_last = bc[C - 1:C, :]
    masks = _head_masks()
    G = _head_sum_matrix()
    S = s_ref[...]

    o = _mm_nt(qs * jnp.exp(bc), S)
    o = o + _mm_ones(qs * kx, G) * v

    ti = lax.broadcasted_iota(jnp.int32, (C, C), 0)
    si = lax.broadcasted_iota(jnp.int32, (C, C), 1)
    A = [jnp.zeros((C, C), F32) for _ in range(H_GRP)]
    bs = C
    while bs >= 2:
        hs = bs // 2
        sel = (si == (ti // bs) * bs + hs - 1).astype(F32)
        bcm = _select_rows(sel, bc)
        qt = qs * jnp.exp(jnp.minimum(bc - bcm, 0.0))
        kt = kx * jnp.exp(jnp.minimum(bcm - bc, 0.0))
        lvl = ((ti // bs) == (si // bs)) & ((ti % bs) >= hs) & ((si % bs) < hs)
        for h in range(H_GRP):
            A[h] = A[h] + jnp.where(lvl, _mm_nt(qt * masks[h], kt), 0.0)
        bs = hs
    for h in range(H_GRP):
        o = o + _mm(A[h], v) * masks[h]

    upd = _mm_tn(v, kx * jnp.exp(b_last - bc))
    ri = lax.broadcasted_iota(jnp.int32, (C_GRP, C_GRP), 0) // HEAD_DIM
    ci = lax.broadcasted_iota(jnp.int32, (C_GRP, C_GRP), 1) // HEAD_DIM
    s_ref[...] = S * jnp.exp(b_last) + jnp.where(ri == ci, upd, 0.0)

    ms = _mm_ones(o * o, G) * (1.0 / HEAD_DIM)
    o_ref[...] = o * lax.rsqrt(ms + RMS_EPS) * ng_ref[...] * (g * _sigmoid(g))


def _hgrn_call(p, lb, ng, B, Lp):
    C = SCAN_CHUNK
    nC = Lp // C
    vec = pl.BlockSpec((1, C_GRP), lambda b, c: (0, 0))
    return pl.pallas_call(
        _hgrn_kernel,
        grid=(B, nC),
        in_specs=[pl.BlockSpec((C, W_HGRN), lambda b, c: (b * nC + c, 0)), vec, vec],
        out_specs=pl.BlockSpec((C, C_GRP), lambda b, c: (b * nC + c, 0)),
        out_shape=jax.ShapeDtypeStruct((B * Lp, C_GRP), F32),
        scratch_shapes=[pltpu.VMEM((C_GRP, C_GRP), F32)],
        compiler_params=_cparams(("parallel", "arbitrary")),
        name="hgrn2",
    )(p, lb, ng)


def _rope_tables(Lp, width, head, rd):
    half = rd // 2
    pos = np.arange(Lp, dtype=np.float32)
    inv_freq = jnp.asarray(ROPE_THETA, F32) ** (-jnp.arange(half, dtype=F32) / half)
    ang = jnp.asarray(pos)[:, None] * inv_freq[None, :]
    cos, sin = jnp.cos(ang), jnp.sin(ang)
    d = np.arange(width) % head
    fi = np.where(d < half, d, d - half) % half
    is1 = jnp.asarray(d < half)
    is2 = jnp.asarray((d >= half) & (d < rd))
    cosl, sinl = cos[:, fi], sin[:, fi]
    ct = jnp.where(is1 | is2, cosl, 1.0)
    s1 = jnp.where(is1, -sinl, 0.0)
    s2 = jnp.where(is2, sinl, 0.0)
    return ct, s1, s2


def _rope(x, ct, s1, s2, half):
    n = x.shape[1]
    return x * ct + pltpu.roll(x, n - half, axis=1) * s1 + pltpu.roll(x, half, axis=1) * s2


def _dsa_prep_kernel(p_ref, cq_ref, s1q_ref, s2q_ref, ci_ref, s1i_ref, s2i_ref,
                     ck_ref, s1k_ref, s2k_ref, eq_ref, ek_ref,
                     q_out, k_out, v_out, qs_out, ks_out, kw_out):
    cq, s1q, s2q = cq_ref[...], s1q_ref[...], s2q_ref[...]
    q = _rope(p_ref[:, 0:256], cq, s1q, s2q, 8)
    q_out[...] = (q * (HEAD_DIM ** -0.5 * math.log2(math.e))).astype(BF16)
    k_out[...] = _rope(p_ref[:, 256:512], cq, s1q, s2q, 8).astype(BF16)
    v_out[...] = p_ref[:, 512:768].astype(BF16)
    qi = _rope(p_ref[:, 768:1024], ci_ref[...], s1i_ref[...], s2i_ref[...], 4)
    kw = _rope(p_ref[:, 1024:1152], ck_ref[...], s1k_ref[...], s2k_ref[...], 4)
    kw_out[...] = kw
    q1, q2, q3 = _split3(qi)
    qs = jnp.dot(q1, eq_ref[0], preferred_element_type=F32)
    qs += jnp.dot(q2, eq_ref[1], preferred_element_type=F32)
    qs += jnp.dot(q3, eq_ref[2], preferred_element_type=F32)
    qs_out[...] = qs.astype(BF16)
    k1, k2, k3 = _split3(kw)
    ks = jnp.dot(k1, ek_ref[0], preferred_element_type=F32)
    ks += jnp.dot(k2, ek_ref[1], preferred_element_type=F32)
    ks += jnp.dot(k3, ek_ref[2], preferred_element_type=F32)
    ks_out[...] = ks.astype(BF16)


def _split_expanders():
    q_slots = ((0, 1, 3), (2, 4), (5,))
    k_slots = ((0, 2, 5), (1, 4), (3,))
    eq = np.zeros((3, IDX_HEADS * IDX_DIM, IDX_HEADS * 256), np.float32)
    ek = np.zeros((3, 128, 256), np.float32)
    for part in range(3):
        for j in q_slots[part]:
            for h in range(IDX_HEADS):
                for d in range(IDX_DIM):
                    eq[part, h * IDX_DIM + d, h * 256 + j * IDX_DIM + d] = 1.0
        for j in k_slots[part]:
            for d in range(IDX_DIM):
                ek[part, d, j * IDX_DIM + d] = 1.0
    return jnp.asarray(eq, BF16), jnp.asarray(ek, BF16)


def _dsa_prep_call(p, B, Lp, tm):
    T = B * Lp
    nL = Lp // tm
    tq = _rope_tables(Lp, 256, HEAD_DIM, HEAD_DIM // 4)
    ti = _rope_tables(Lp, 256, IDX_DIM, IDX_DIM // 4)
    tk = tuple(jnp.concatenate([t, jnp.full((Lp, 96), fill, F32)], axis=1)
               for t, fill in zip(_rope_tables(Lp, 32, IDX_DIM, IDX_DIM // 4), (1.0, 0.0, 0.0)))
    eq, ek = _split_expanders()
    tab = lambda n: pl.BlockSpec((tm, n), lambda i: (i % nL, 0))
    row = lambda n: pl.BlockSpec((tm, n), lambda i: (i, 0))
    outs = [(256, BF16), (256, BF16), (256, BF16), (IDX_HEADS * 256, BF16), (256, BF16), (128, F32)]
    return pl.pallas_call(
        _dsa_prep_kernel,
        grid=(T // tm,),
        in_specs=[row(W_DSA)] + [tab(256)] * 6 + [tab(128)] * 3
        + [pl.BlockSpec(eq.shape, lambda i: (0, 0, 0)), pl.BlockSpec(ek.shape, lambda i: (0, 0, 0))],
        out_specs=[row(n) for n, _ in outs],
        out_shape=[jax.ShapeDtypeStruct((T, n), dt) for n, dt in outs],
        compiler_params=_cparams(("parallel",)),
        name="dsa_prep",
    )(p, *tq, *ti, *tk, eq, ek)


def _dsa_kernel(q_ref, qs_ref, w_ref, k_ref, ks_ref, vt_ref, o_ref,
                key_ref, k16_ref, acc_ref, s_ref, p_ref, *, topk):
    TQ, KB = DSA_TQ, DSA_KB
    i = pl.program_id(1)
    q0 = i * TQ
    nkb = (q0 + TQ + KB - 1) // KB
    qpos = q0 + lax.broadcasted_iota(jnp.int32, (1, TQ), 1)
    krow = lax.broadcasted_iota(jnp.int32, (KB, 1), 0)
    nt = (((1,), (1,)), ((), ()))

    def kslice(j):
        return pl.ds(pl.multiple_of(j * KB, KB), KB)

    w = w_ref[...] * (IDX_HEADS ** -0.5 * IDX_DIM ** -0.5)

    def idx_body(j, carry):
        ks = ks_ref[kslice(j), :]
        acc = jnp.zeros((KB, TQ), F32)
        for h in range(IDX_HEADS):
            s = lax.dot_general(ks, qs_ref[:, h * 256:(h + 1) * 256], nt, preferred_element_type=F32)
            acc = acc + w[h:h + 1, :] * jnp.maximum(s, 0.0)
        acc = jnp.where(j * KB + krow <= qpos, acc, NEG_INF)
        bits = pltpu.bitcast(acc, jnp.int32)
        key = bits ^ ((bits >> 31) & jnp.int32(0x7FFFFFFF))
        key_ref[kslice(j), :] = key
        k16_ref[kslice(j), :] = (key >> 16).astype(jnp.int16)
        return carry

    lax.fori_loop(0, nkb, idx_body, 0)

    def count_ge(cand):
        def body(j, cnt):
            hit = (key_ref[kslice(j), :] >= cand).astype(jnp.int32)
            return cnt + jnp.sum(hit.reshape(KB // 8, 8, TQ), axis=0)
        cnt = lax.fori_loop(0, nkb, body, jnp.zeros((8, TQ), jnp.int32))
        return jnp.sum(cnt, axis=0, keepdims=True)

    def count_ge16(cand):
        c16 = cand.astype(jnp.int16)
        def body(j, cnt):
            hit = (k16_ref[kslice(j), :] >= c16).astype(jnp.int16)
            parts = [hit[r:r + 16] for r in range(0, KB, 16)]
            while len(parts) > 1:
                parts = [a + b for a, b in zip(parts[::2], parts[1::2])]
            return cnt + parts[0]
        cnt = lax.fori_loop(0, nkb, body, jnp.zeros((16, TQ), jnp.int16))
        return jnp.sum(cnt.astype(jnp.int32), axis=0, keepdims=True)

    def select16(want):
        def bit_body(b, res_u):
            cand_u = res_u | (jnp.int32(1) << (15 - b))
            cnt = count_ge16(cand_u - HALF)
            return jnp.where(cnt >= want, cand_u, res_u)
        return lax.fori_loop(0, 16, bit_body, jnp.zeros((1, TQ), jnp.int32))

    HALF = 32768
    hi = select16(topk) - HALF
    cnt_above = jnp.where(hi == HALF - 1, 0, count_ge16(jnp.minimum(hi + 1, HALF - 1)))
    hi16 = hi.astype(jnp.int16)

    def low_body(j, carry):
        lo = ((key_ref[kslice(j), :] & jnp.int32(0xFFFF)) - HALF).astype(jnp.int16)
        k16_ref[kslice(j), :] = jnp.where(k16_ref[kslice(j), :] == hi16, lo, jnp.int16(-HALF))
        return carry

    lax.fori_loop(0, nkb, low_body, 0)
    thr = (hi << 16) | select16(topk - cnt_above)
    cnt_ge = count_ge(thr)
    cnt_gt = count_ge(thr + 1)
    n_take = topk - cnt_gt
    neg_key = jnp.int32(np.array(NEG_INF, np.float32).view(np.int32)) ^ jnp.int32(0x7FFFFFFF)
    need_rank = jnp.any(((cnt_ge - cnt_gt) > n_take) & (thr > neg_key))

    def put_bias(j, sel):
        key_ref[kslice(j), :] = pltpu.bitcast(jnp.where(sel, 0.0, NEG_INF), jnp.int32)

    def fast_body(j, carry):
        put_bias(j, (key_ref[kslice(j), :] >= thr) & (j * KB + krow <= qpos))
        return carry

    def rank_body(j, seen):
        blk = key_ref[kslice(j), :]
        eq = blk == thr
        ri = lax.broadcasted_iota(jnp.int32, (KB, KB), 0)
        ci = lax.broadcasted_iota(jnp.int32, (KB, KB), 1)
        lower = (ci <= ri).astype(BF16)
        rank = seen + jnp.dot(lower, eq.astype(BF16), preferred_element_type=F32)
        put_bias(j, ((blk > thr) | (eq & (rank <= n_take.astype(F32)))) & (j * KB + krow <= qpos))
        return seen + jnp.sum(eq.astype(F32), axis=0, keepdims=True)

    @pl.when(jnp.logical_not(need_rank))
    def _():
        lax.fori_loop(0, nkb, fast_body, 0)

    @pl.when(need_rank)
    def _():
        lax.fori_loop(0, nkb, rank_body, jnp.zeros((1, TQ), F32))

    lane_head = lax.broadcasted_iota(jnp.int32, (1, C_GRP), 1) // HEAD_DIM
    q = q_ref[...]
    qh = [jnp.where(lane_head == h, q, jnp.zeros_like(q)) for h in range(H_GRP)]

    def scores(j, h):
        bias = pltpu.bitcast(key_ref[kslice(j), :], F32)
        return lax.dot_general(k_ref[kslice(j), :], qh[h], nt, preferred_element_type=F32) + bias

    def max_body(j, ms):
        return tuple(jnp.maximum(m, jnp.max(scores(j, h).reshape(KB // 8, 8, TQ), axis=0))
                     for h, m in enumerate(ms))

    ms = lax.fori_loop(0, nkb, max_body, tuple(jnp.full((8, TQ), NEG_INF, F32) for _ in range(H_GRP)))
    ms = [jnp.maximum(jnp.max(m, axis=0, keepdims=True), 0.1 * NEG_INF) for m in ms]

    acc_ref[...] = jnp.zeros(acc_ref.shape, F32)

    def pv_body(j, ls):
        for h in range(H_GRP):
            s_ref[h] = scores(j, h)
        out = []
        for h in range(H_GRP):
            pexp = jnp.exp2(s_ref[h] - ms[h])
            out.append(ls[h] + jnp.sum(pexp.reshape(KB // 8, 8, TQ), axis=0))
            p_ref[h] = pexp.astype(BF16)
        for h in range(H_GRP):
            hs = slice(h * HEAD_DIM, (h + 1) * HEAD_DIM)
            acc_ref[hs, :] += jnp.dot(vt_ref[hs, kslice(j)], p_ref[h], preferred_element_type=F32)
        return tuple(out)

    ls = lax.fori_loop(0, nkb, pv_body, tuple(jnp.zeros((8, TQ), F32) for _ in range(H_GRP)))
    for h in range(H_GRP):
        hs = slice(h * HEAD_DIM, (h + 1) * HEAD_DIM)
        acc_ref[hs, :] = acc_ref[hs, :] / jnp.sum(ls[h], axis=0, keepdims=True)
    o_ref[...] = acc_ref[...].T


def _dsa_call(q, qs, wt, k3, ks3, vt3, B, Lp, topk):
    TQ = DSA_TQ
    nQ = Lp // TQ
    rowq = lambda n: pl.BlockSpec((TQ, n), lambda b, i: (b * nQ + i, 0))
    full = lambda r, c: pl.BlockSpec((None, r, c), lambda b, i: (b, 0, 0))
    return pl.pallas_call(
        functools.partial(_dsa_kernel, topk=topk),
        grid=(B, nQ),
        in_specs=[rowq(256), rowq(IDX_HEADS * 256),
                  pl.BlockSpec((None, IDX_HEADS, TQ), lambda b, i: (b, 0, i)),
                  full(Lp, 256), full(Lp, 256), full(256, Lp)],
        out_specs=rowq(C_GRP),
        out_shape=jax.ShapeDtypeStruct((B * Lp, C_GRP), F32),
        scratch_shapes=[pltpu.VMEM((Lp, TQ), jnp.int32),
                        pltpu.VMEM((Lp, TQ), jnp.int16),
                        pltpu.VMEM((C_GRP, TQ), F32),
                        pltpu.VMEM((H_GRP, DSA_KB, TQ), F32),
                        pltpu.VMEM((H_GRP, DSA_KB, TQ), BF16)],
        compiler_params=_cparams(("parallel", "arbitrary")),
        name="dsa_attn",
    )(q, qs, wt, k3, ks3, vt3)


def _moe_kernel(x_ref, rw_ref, rb_ref, w1_ref, b1_ref, w2_ref, b2_ref, g_ref, b_ref, o_ref,
                gate_ref, acc_ref):
    e = pl.program_id(1)
    tm = x_ref.shape[0]
    x = x_ref[...]
    lane = lax.broadcasted_iota(jnp.int32, (tm, N_EXPERTS), 1)

    @pl.when(e == 0)
    def _():
        logits = _mmh(x, rw_ref[...]) + rb_ref[...]
        work = logits
        vals, hots = [], []
        for _ in range(TOP_K):
            mx = jnp.max(work, axis=1, keepdims=True)
            first = jnp.min(jnp.where(work == mx, lane, N_EXPERTS), axis=1, keepdims=True)
            hot = lane == first
            vals.append(mx)
            hots.append(hot)
            work = jnp.where(hot, -jnp.inf, work)
        ex = [jnp.exp(vv - vals[0]) for vv in vals]
        den = ex[0] + ex[1] + ex[2] + ex[3]
        gates = jnp.zeros((tm, N_EXPERTS), F32)
        for hot, w in zip(hots, ex):
            gates = gates + jnp.where(hot, w / den, 0.0)
        gate_ref[...] = gates
        acc_ref[...] = jnp.zeros_like(acc_ref)

    xb = x.astype(BF16)
    gate_e = jnp.sum(jnp.where(lane == e, gate_ref[...], 0.0), axis=1, keepdims=True)
    DE = D_MODEL
    y = jnp.zeros((tm, D_MODEL), F32) + b2_ref[0]
    nj = 4
    wj = DE // nj
    for j in range(nj):
        glu = jnp.dot(xb, w1_ref[0, :, j * wj:(j + 1) * wj], preferred_element_type=F32)
        glu = jnp.minimum(glu + b1_ref[0, :, j * wj:(j + 1) * wj], SWIGLU_LIMIT)
        lin = jnp.dot(xb, w1_ref[0, :, DE + j * wj:DE + (j + 1) * wj], preferred_element_type=F32)
        lin = jnp.clip(lin + b1_ref[0, :, DE + j * wj:DE + (j + 1) * wj], -SWIGLU_LIMIT, SWIGLU_LIMIT)
        act = glu * _sigmoid(SWIGLU_ALPHA * glu) * (lin + 1.0)
        y = y + jnp.dot(act.astype(BF16), w2_ref[0, j * wj:(j + 1) * wj, :], preferred_element_type=F32)
    acc_ref[...] += gate_e * y

    @pl.when(e == N_EXPERTS - 1)
    def _():
        o_ref[...] = _layernorm_rows(DEEPNORM_ALPHA * x + acc_ref[...], g_ref[...], b_ref[...])


def _moe_call(h, rw, rb, w1, b1, w2, b2, g, b, tm):
    T, D = h.shape
    E = N_EXPERTS
    return pl.pallas_call(
        _moe_kernel,
        grid=(T // tm, E),
        in_specs=[pl.BlockSpec((tm, D), lambda i, e: (i, 0)),
                  pl.BlockSpec((D, E), lambda i, e: (0, 0)),
                  pl.BlockSpec((1, E), lambda i, e: (0, 0)),
                  pl.BlockSpec((1, D, 2 * D), lambda i, e: (e, 0, 0)),
                  pl.BlockSpec((1, 1, 2 * D), lambda i, e: (e, 0, 0)),
                  pl.BlockSpec((1, D, D), lambda i, e: (e, 0, 0)),
                  pl.BlockSpec((1, 1, D), lambda i, e: (e, 0, 0)),
                  pl.BlockSpec((1, D), lambda i, e: (0, 0)),
                  pl.BlockSpec((1, D), lambda i, e: (0, 0))],
        out_specs=pl.BlockSpec((tm, D), lambda i, e: (i, 0)),
        out_shape=jax.ShapeDtypeStruct((T, D), F32),
        scratch_shapes=[pltpu.VMEM((tm, E), F32), pltpu.VMEM((tm, D), F32)],
        compiler_params=_cparams(("parallel", "arbitrary")),
        name="moe",
    )(h, rw, rb.reshape(1, E), w1, b1.reshape(E, 1, 2 * D), w2, b2.reshape(E, 1, D),
      g.reshape(1, D), b.reshape(1, D))


def _regroup_w_in(w):
    D = w.shape[0]
    z = lambda n: jnp.zeros((D, n), w.dtype)
    o = 0
    r, wd, k, v, ad, gd = (w[:, a:b] for a, b in _bounds(o, (256, 64, 256, 256, 64, 64)))
    o += 960
    dsa = w[:, o:o + 1064]
    o += 1064
    lru = w[:, o:o + 512]
    o += 512
    hg = w[:, o:o + 1024]
    return jnp.concatenate([r, k, v, wd, ad, gd, z(64), dsa, z(W_DSA - 1064), lru, hg], axis=1).astype(BF16)


def _bounds(start, sizes):
    out = []
    for s in sizes:
        out.append((start, start + s))
        start += s
    return out


def _regroup_mu(mu):
    r, wd, k, v, ad, gd = (mu[a:b] for a, b in _bounds(0, (256, 64, 256, 256, 64, 64)))
    return jnp.concatenate([r, k, v, wd, ad, gd, jnp.zeros((64,), mu.dtype)]).reshape(1, W_RWKV)


def _lora_pad(w, slot):
    return jnp.zeros((C_GRP, C_GRP), w.dtype).at[slot * LORA:(slot + 1) * LORA].set(w).astype(BF16)


def _block_diag(w):
    out = jnp.zeros((C_GRP, C_GRP), w.dtype)
    for h in range(H_GRP):
        out = out.at[h * HEAD_DIM:(h + 1) * HEAD_DIM, h * HEAD_DIM:(h + 1) * HEAD_DIM].set(w[h])
    return out.astype(BF16)


def _pick_tile(T, pref):
    for tm in (pref, 512, 256, 128, 64, 32, 16, 8):
        if tm <= pref and T % tm == 0:
            return tm
    raise ValueError(f"no row tile for {T}")


def kernel(x, meta_tokens, ln_in_g, ln_in_b, w_in, rwkv_mu, rwkv_w0, rwkv_w2, rwkv_a0, rwkv_a2,
           rwkv_g2, rwkv_k_k, rwkv_k_a, rwkv_r_k, rwkv_ln_g, rwkv_ln_b, lru_conv_w, lru_conv_b,
           lru_wa, lru_ba, lru_wx, lru_bx, lru_lambda, hgrn_lower_bounds, hgrn_norm_g, w_out,
           ln_mix_g, ln_mix_b, router_w, router_b, moe_w1, moe_b1, moe_w2, moe_b2,
           ln_ffn_g, ln_ffn_b):
    B, S, D = x.shape
    assert D == D_MODEL
    depth = w_in.shape[0]
    topk = min(TOPK_MAX, S // 4)
    L = S + N_META
    Lp = -(-L // SEQ_ALIGN) * SEQ_ALIGN
    T = B * Lp
    tm = _pick_tile(T, 512)
    v1 = lambda t: t.reshape(1, -1)

    h = jnp.concatenate([jnp.broadcast_to(meta_tokens[None].astype(x.dtype), (B, N_META, D)), x,
                         jnp.zeros((B, Lp - L, D), x.dtype)], axis=1).reshape(T, D)
    h = _ln_call(h, ln_in_g, ln_in_b, tm)

    s_lb = jax.nn.softmax(hgrn_lower_bounds.astype(F32), axis=0)
    lower_bounds = jnp.cumsum(s_lb, axis=0) - s_lb[0]

    for l in range(depth):
        p_rwkv, p_dsa, p_lru, p_hgrn = _proj_call(h, _regroup_w_in(w_in[l]), tm)

        y_rwkv = _rwkv_call(p_rwkv, (
            _regroup_mu(rwkv_mu[l]), v1(rwkv_w0[l]), _lora_pad(rwkv_w2[l], 0), v1(rwkv_a0[l]),
            _lora_pad(rwkv_a2[l], 1), _lora_pad(rwkv_g2[l], 2), v1(rwkv_k_k[l]), v1(rwkv_k_a[l]),
            v1(rwkv_r_k[l]), v1(rwkv_ln_g[l]), v1(rwkv_ln_b[l])), B, Lp)

        q, k, v, qs, ks, kw = _dsa_prep_call(p_dsa, B, Lp, _pick_tile(Lp, 512))
        wt = jnp.swapaxes(kw.reshape(B, Lp, 128)[:, :, IDX_DIM:IDX_DIM + IDX_HEADS], 1, 2)
        vt = jnp.swapaxes(v.reshape(B, Lp, 256), 1, 2)
        y_dsa = _dsa_call(q, qs, wt, k.reshape(B, Lp, 256), ks.reshape(B, Lp, 256), vt, B, Lp, topk)

        y_lru = _lru_call(p_lru, (lru_conv_w[l], v1(lru_conv_b[l]), _block_diag(lru_wa[l]),
                                  v1(lru_ba[l]), _block_diag(lru_wx[l]), v1(lru_bx[l]),
                                  v1(lru_lambda[l])), B, Lp)

        y_hgrn = _hgrn_call(p_hgrn, v1(lower_bounds[l]), v1(hgrn_norm_g[l]), B, Lp)

        h = _mix_call(h, (y_rwkv, y_dsa, y_lru, y_hgrn), w_out[l].astype(BF16),
                      ln_mix_g[l], ln_mix_b[l], tm)
        h = _moe_call(h, router_w[l], router_b[l], moe_w1[l].astype(BF16), moe_b1[l],
                      moe_w2[l].astype(BF16), moe_b2[l], ln_ffn_g[l], ln_ffn_b[l], tm)

    return h.reshape(B, Lp, D)[:, N_META:L]
```

```python
import functools
import math

import numpy as np
import jax
import jax.numpy as jnp
from jax import lax
from jax.experimental import pallas as pl
from jax.experimental.pallas import tpu as pltpu

F32 = jnp.float32
BF16 = jnp.bfloat16
HIGHEST = lax.Precision.HIGHEST

D_MODEL = 1024
N_META = 16
HEAD_DIM = 64
C_GRP = 256
H_GRP = 4
LORA = 64
IDX_HEADS = 8
IDX_DIM = 32
TOPK_MAX = 256
ROPE_THETA = 500000.0
LRU_C = 8.0
CONV_WIDTH = 4
LN_EPS = 1e-5
RMS_EPS = 1e-5
RWKV_LN_EPS = 64e-5
NEG_INF = -1e30
N_EXPERTS = 32
TOP_K = 4
SWIGLU_ALPHA = 1.702
SWIGLU_LIMIT = 7.0
DEPTH = 2
DEEPNORM_ALPHA = (2 * DEPTH) ** 0.25

SEQ_ALIGN = 256
RWKV_CHUNK = 64
RWKV_ROWS = 2
SCAN_CHUNK = 128
DSA_TQ = 256
DSA_KB = 256
VMEM_LIMIT = 56 * 1024 * 1024

W_RWKV = 1024
W_DSA = 1152
W_LRU = 512
W_HGRN = 1024


def _cparams(sem):
    return pltpu.CompilerParams(dimension_semantics=sem, vmem_limit_bytes=VMEM_LIMIT)


def _mm(a, b):
    return jnp.dot(a.astype(BF16), b.astype(BF16), preferred_element_type=F32)


def _mm_nt(a, b):
    return lax.dot_general(a.astype(BF16), b.astype(BF16), (((1,), (1,)), ((), ())),
                           preferred_element_type=F32)


def _mm_tn(a, b):
    return lax.dot_general(a.astype(BF16), b.astype(BF16), (((0,), (0,)), ((), ())),
                           preferred_element_type=F32)


def _mmh(a, b):
    return jnp.dot(a, b, precision=HIGHEST, preferred_element_type=F32)


def _split2(x):
    hi = x.astype(BF16)
    return hi, (x - hi.astype(F32)).astype(BF16)


def _split3(x):
    x1 = x.astype(BF16)
    r1 = x - x1.astype(F32)
    x2 = r1.astype(BF16)
    x3 = (r1 - x2.astype(F32)).astype(BF16)
    return x1, x2, x3


def _mm3(a, b):
    ah, al = _split2(a)
    bh, bl = _split2(b)
    d = lambda x, y: jnp.dot(x, y, preferred_element_type=F32)
    return d(ah, bh) + (d(ah, bl) + d(al, bh))


def _mm_ones(a, g):
    hi, lo = _split2(a)
    gb = g.astype(BF16)
    return (jnp.dot(hi, gb, preferred_element_type=F32) + jnp.dot(lo, gb, preferred_element_type=F32))


def _select_rows(sel, x):
    x1, x2, x3 = _split3(x)
    sb = sel.astype(BF16)
    return (jnp.dot(sb, x1, preferred_element_type=F32) + jnp.dot(sb, x2, preferred_element_type=F32)
            + jnp.dot(sb, x3, preferred_element_type=F32))


def _sigmoid(x):
    return 1.0 / (1.0 + jnp.exp(-x))


def _softplus(x):
    return jnp.maximum(x, 0.0) + jnp.log1p(jnp.exp(-jnp.abs(x)))


def _head_masks():
    lane = lax.broadcasted_iota(jnp.int32, (1, C_GRP), 1) // HEAD_DIM
    return [(lane == h).astype(F32) for h in range(H_GRP)]


def _head_sum_matrix():
    r = lax.broadcasted_iota(jnp.int32, (C_GRP, C_GRP), 0) // HEAD_DIM
    c = lax.broadcasted_iota(jnp.int32, (C_GRP, C_GRP), 1) // HEAD_DIM
    return (r == c).astype(F32)


def _cumsum_rows(x):
    n = x.shape[0]
    row = lax.broadcasted_iota(jnp.int32, (n, 1), 0)
    s = 1
    while s < n:
        x = x + jnp.where(row >= s, pltpu.roll(x, s, axis=0), 0.0)
        s *= 2
    return x


def _layernorm_rows(x, g, b):
    mu = jnp.mean(x, axis=-1, keepdims=True)
    xc = x - mu
    var = jnp.mean(xc * xc, axis=-1, keepdims=True)
    return xc * lax.rsqrt(var + LN_EPS) * g + b


def _ln_kernel(x_ref, g_ref, b_ref, o_ref):
    o_ref[...] = _layernorm_rows(x_ref[...], g_ref[...], b_ref[...])


def _ln_call(x, g, b, tm):
    T, D = x.shape
    return pl.pallas_call(
        _ln_kernel,
        grid=(T // tm,),
        in_specs=[pl.BlockSpec((tm, D), lambda i: (i, 0)),
                  pl.BlockSpec((1, D), lambda i: (0, 0)),
                  pl.BlockSpec((1, D), lambda i: (0, 0))],
        out_specs=pl.BlockSpec((tm, D), lambda i: (i, 0)),
        out_shape=jax.ShapeDtypeStruct((T, D), F32),
        compiler_params=_cparams(("parallel",)),
        name="ln_in",
    )(x, g.reshape(1, D), b.reshape(1, D))


def _proj_kernel(h_ref, w_ref, o_rwkv, o_dsa, o_lru, o_hgrn):
    hb = h_ref[...].astype(BF16)
    off = 0
    for o, n in ((o_rwkv, W_RWKV), (o_dsa, W_DSA), (o_lru, W_LRU), (o_hgrn, W_HGRN)):
        o[...] = jnp.dot(hb, w_ref[:, off:off + n], preferred_element_type=F32)
        off += n


def _proj_call(h, w, tm):
    T, D = h.shape
    widths = (W_RWKV, W_DSA, W_LRU, W_HGRN)
    return pl.pallas_call(
        _proj_kernel,
        grid=(T // tm,),
        in_specs=[pl.BlockSpec((tm, D), lambda i: (i, 0)),
                  pl.BlockSpec((D, sum(widths)), lambda i: (0, 0))],
        out_specs=[pl.BlockSpec((tm, n), lambda i: (i, 0)) for n in widths],
        out_shape=[jax.ShapeDtypeStruct((T, n), F32) for n in widths],
        compiler_params=_cparams(("parallel",)),
        name="proj_in",
    )(h, w)


def _mix_kernel(h_ref, y0, y1, y2, y3, w_ref, g_ref, b_ref, rwt_ref, rb_ref, o_ref, ob_ref, gt_ref):
    mix = _mm(y0[...], w_ref[0:C_GRP, :])
    mix += _mm(y1[...], w_ref[C_GRP:2 * C_GRP, :])
    mix += _mm(y2[...], w_ref[2 * C_GRP:3 * C_GRP, :])
    mix += _mm(y3[...], w_ref[3 * C_GRP:4 * C_GRP, :])
    h = _layernorm_rows(DEEPNORM_ALPHA * h_ref[...] + mix, g_ref[...], b_ref[...])
    o_ref[...] = h
    ob_ref[...] = h.astype(BF16)

    tm = h.shape[0]
    logits = lax.dot_general(rwt_ref[...], h, (((1,), (1,)), ((), ())), precision=HIGHEST,
                             preferred_element_type=F32) + rb_ref[...]
    row = lax.broadcasted_iota(jnp.int32, (N_EXPERTS, tm), 0)
    work = logits
    vals, hots = [], []
    for _ in range(TOP_K):
        mx = jnp.max(work, axis=0, keepdims=True)
        first = jnp.min(jnp.where(work == mx, row, N_EXPERTS), axis=0, keepdims=True)
        hot = row == first
        vals.append(mx)
        hots.append(hot)
        work = jnp.where(hot, -jnp.inf, work)
    ex = [jnp.exp(vv - vals[0]) for vv in vals]
    den = ex[0] + ex[1] + ex[2] + ex[3]
    gates = jnp.zeros((N_EXPERTS, tm), F32)
    for hot, w in zip(hots, ex):
        gates = gates + jnp.where(hot, w / den, 0.0)
    gt_ref[...] = gates


def _mix_call(h, ys, w, g, b, rw, rb, tm):
    T, D = h.shape
    E = N_EXPERTS
    return pl.pallas_call(
        _mix_kernel,
        grid=(T // tm,),
        in_specs=[pl.BlockSpec((tm, D), lambda i: (i, 0))]
        + [pl.BlockSpec((tm, C_GRP), lambda i: (i, 0))] * 4
        + [pl.BlockSpec((4 * C_GRP, D), lambda i: (0, 0)),
           pl.BlockSpec((1, D), lambda i: (0, 0)),
           pl.BlockSpec((1, D), lambda i: (0, 0)),
           pl.BlockSpec((E, D), lambda i: (0, 0)),
           pl.BlockSpec((E, 1), lambda i: (0, 0))],
        out_specs=[pl.BlockSpec((tm, D), lambda i: (i, 0)),
                   pl.BlockSpec((tm, D), lambda i: (i, 0)),
                   pl.BlockSpec((E, tm), lambda i: (0, i))],
        out_shape=[jax.ShapeDtypeStruct((T, D), F32), jax.ShapeDtypeStruct((T, D), BF16),
                   jax.ShapeDtypeStruct((E, T), F32)],
        compiler_params=_cparams(("parallel",)),
        name="mix_out",
    )(h, *ys, w, g.reshape(1, D), b.reshape(1, D), rw.T, rb.reshape(E, 1))


def _rwkv_kernel(p_ref, mu_ref, w0_ref, w2_ref, a0_ref, a2_ref, g2_ref, kk_ref, ka_ref, rk_ref,
                 lng_ref, lnb_ref, o_ref, prev_ref, s_ref):
    @pl.when(pl.program_id(1) == 0)
    def _():
        prev_ref[...] = jnp.zeros_like(prev_ref)
        s_ref[...] = jnp.zeros_like(s_ref)

    for bb in range(p_ref.shape[0]):
        _rwkv_chunk(p_ref.at[bb], mu_ref, w0_ref, w2_ref, a0_ref, a2_ref, g2_ref, kk_ref, ka_ref,
                    rk_ref, lng_ref, lnb_ref, o_ref.at[bb], prev_ref.at[bb], s_ref.at[bb])


def _rwkv_chunk(p_ref, mu_ref, w0_ref, w2_ref, a0_ref, a2_ref, g2_ref, kk_ref, ka_ref, rk_ref,
                lng_ref, lnb_ref, o_ref, prev_ref, s_ref):
    C = RWKV_CHUNK
    p = p_ref[...]
    row = lax.broadcasted_iota(jnp.int32, (C, 1), 0)
    p_prev = jnp.where(row == 0, prev_ref[0:1, :], pltpu.roll(p, 1, axis=0))
    prev_ref[0:1, :] = p[C - 1:C, :]
    ps = p + (p_prev - p) * mu_ref[...]
    r = ps[:, 0:256]
    k = ps[:, 256:512]
    v = ps[:, 512:768]
    lora = ps[:, 768:1024]

    w = -_softplus(-(w0_ref[...] + _mm(jnp.tanh(lora), w2_ref[...]))) - 0.5
    logw = -jnp.exp(w)
    a = _sigmoid(a0_ref[...] + _mm(lora, a2_ref[...]))
    g = _mm(_sigmoid(lora), g2_ref[...])

    G = _head_sum_matrix()
    kk = k * kk_ref[...]
    kk = kk / jnp.maximum(jnp.sqrt(_mm_ones(kk * kk, G)), 1e-12)
    k_mod = k * (1.0 + (a - 1.0) * ka_ref[...])
    a_vec = -kk
    b_vec = kk * a

    cum = _cumsum_rows(logw)
    cum_last = cum[C - 1:C, :]
    e_inv = jnp.exp(-cum)
    g_end = jnp.exp(cum_last - cum)
    masks = _head_masks()

    def stack(x):
        return jnp.concatenate([x * m for m in masks], axis=0)

    Am = stack(a_vec * jnp.exp(cum - logw))
    Rm = stack(r * jnp.exp(cum))
    Bm = stack(b_vec * e_inv)
    Km = stack(k_mod * e_inv)
    Vs = stack(v)
    BKg = jnp.concatenate([stack(b_vec * g_end), stack(k_mod * g_end)], axis=0)

    ri = lax.broadcasted_iota(jnp.int32, (4 * C, 4 * C), 0)
    ci = lax.broadcasted_iota(jnp.int32, (4 * C, 4 * C), 1)
    same = (ri // C) == (ci // C)
    strict = same & (ci < ri)
    incl = same & (ci <= ri)

    N = jnp.where(strict, _mm_nt(Am, Bm), 0.0)
    Aak = jnp.where(strict, _mm_nt(Am, Km), 0.0)
    Arb = jnp.where(incl, _mm_nt(Rm, Bm), 0.0)
    Ark = jnp.where(incl, _mm_nt(Rm, Km), 0.0)

    base = 8
    Nb = jnp.where((ri // base) == (ci // base), N, 0.0)
    P = jnp.where(ri == ci, 1.0, 0.0) + Nb
    M = Nb
    for _ in range(int(math.log2(base)) - 1):
        M = _mm3(M, M)
        P = P + _mm3(P, M)
    s = base
    while s < C:
        lower_left = ((ri // (2 * s)) == (ci // (2 * s))) & ((ri // s) % 2 == 1) & ((ci // s) % 2 == 0)
        P = P + _mm(P, _mm(jnp.where(lower_left, N, 0.0), P))
        s *= 2

    S = s_ref[...]
    U = _mm(P, _mm_nt(Am, S) + _mm(Aak, Vs))
    Y = _mm_nt(Rm, S) + _mm(Arb, U) + _mm(Ark, Vs)
    y = Y[0:C] + Y[C:2 * C] + Y[2 * C:3 * C] + Y[3 * C:4 * C]

    UV = jnp.concatenate([U, Vs], axis=0)
    s_ref[...] = S * jnp.exp(cum_last) + _mm_tn(UV, BKg)

    inv_n = 1.0 / HEAD_DIM
    y_mu = _mm_ones(y, G) * inv_n
    yc = y - y_mu
    y_var = _mm_ones(yc * yc, G) * inv_n
    yn = yc * lax.rsqrt(y_var + RWKV_LN_EPS) * lng_ref[...] + lnb_ref[...]
    bonus = _mm_ones(r * k_mod * rk_ref[...], G) * v
    o_ref[...] = (yn + bonus) * g


def _rwkv_call(p, prm, B, Lp):
    C = RWKV_CHUNK
    nC = Lp // C
    NB = RWKV_ROWS if B % RWKV_ROWS == 0 else 1
    vec = lambda n: pl.BlockSpec((1, n), lambda b, c: (0, 0))
    mat = pl.BlockSpec((C_GRP, C_GRP), lambda b, c: (0, 0))
    out = pl.pallas_call(
        _rwkv_kernel,
        grid=(B // NB, nC),
        in_specs=[pl.BlockSpec((NB, C, W_RWKV), lambda b, c: (b, c, 0)),
                  vec(W_RWKV), vec(C_GRP), mat, vec(C_GRP), mat, mat,
                  vec(C_GRP), vec(C_GRP), vec(C_GRP), vec(C_GRP), vec(C_GRP)],
        out_specs=pl.BlockSpec((NB, C, C_GRP), lambda b, c: (b, c, 0)),
        out_shape=jax.ShapeDtypeStruct((B, Lp, C_GRP), F32),
        scratch_shapes=[pltpu.VMEM((NB, 8, W_RWKV), F32), pltpu.VMEM((NB, C_GRP, C_GRP), F32)],
        compiler_params=_cparams(("parallel", "arbitrary")),
        name="rwkv7",
    )(p.reshape(B, Lp, W_RWKV), *prm)
    return out.reshape(B * Lp, C_GRP)


def _gelu_tanh(x):
    return 0.5 * x * (1.0 + jnp.tanh(math.sqrt(2.0 / math.pi) * (x + 0.044715 * x * x * x)))


def _lru_kernel(p_ref, cw_ref, cb_ref, wa_ref, ba_ref, wx_ref, bx_ref, lam_ref, o_ref,
                xbuf_ref, h_ref):
    C = SCAN_CHUNK
    c = pl.program_id(1)

    @pl.when(c == 0)
    def _():
        xbuf_ref[0:8, :] = jnp.zeros((8, C_GRP), F32)
        h_ref[...] = jnp.zeros_like(h_ref)

    xbuf_ref[8:8 + C, :] = p_ref[:, 0:C_GRP]
    gate = p_ref[:, C_GRP:2 * C_GRP]
    xc = cb_ref[...] + jnp.zeros((C, C_GRP), F32)
    for j in range(CONV_WIDTH):
        xc = xc + cw_ref[j:j + 1, :] * xbuf_ref[pl.ds(8 - (CONV_WIDTH - 1) + j, C), :]
    xbuf_ref[0:8, :] = xbuf_ref[C:C + 8, :]

    r = _sigmoid(_mm(xc, wa_ref[...]) + ba_ref[...])
    i = _sigmoid(_mm(xc, wx_ref[...]) + bx_ref[...])
    log_a = -LRU_C * r * _softplus(-lam_ref[...])
    a = jnp.exp(log_a)
    th = jnp.tanh(log_a)
    u = xc * i * jnp.sqrt(-2.0 * th / (1.0 - th))

    row = lax.broadcasted_iota(jnp.int32, (C, 1), 0)
    s = 1
    while s < C:
        keep = row >= s
        u = u + a * jnp.where(keep, pltpu.roll(u, s, axis=0), 0.0)
        a = a * jnp.where(keep, pltpu.roll(a, s, axis=0), 1.0)
        s *= 2
    h = u + a * h_ref[0:1, :]
    h_ref[0:1, :] = h[C - 1:C, :]
    o_ref[...] = h * _gelu_tanh(gate)


def _lru_call(p, prm, B, Lp):
    C = SCAN_CHUNK
    nC = Lp // C
    vec = pl.BlockSpec((1, C_GRP), lambda b, c: (0, 0))
    mat = pl.BlockSpec((C_GRP, C_GRP), lambda b, c: (0, 0))
    return pl.pallas_call(
        _lru_kernel,
        grid=(B, nC),
        in_specs=[pl.BlockSpec((C, W_LRU), lambda b, c: (b * nC + c, 0)),
                  pl.BlockSpec((CONV_WIDTH, C_GRP), lambda b, c: (0, 0)),
                  vec, mat, vec, mat, vec, vec],
        out_specs=pl.BlockSpec((C, C_GRP), lambda b, c: (b * nC + c, 0)),
        out_shape=jax.ShapeDtypeStruct((B * Lp, C_GRP), F32),
        scratch_shapes=[pltpu.VMEM((C + 8, C_GRP), F32), pltpu.VMEM((8, C_GRP), F32)],
        compiler_params=_cparams(("parallel", "arbitrary")),
        name="rglru",
    )(p, *prm)


def _hgrn_kernel(p_ref, lb_ref, ng_ref, o_ref, s_ref):
    C = SCAN_CHUNK
    c = pl.program_id(1)

    @pl.when(c == 0)
    def _():
        s_ref[...] = jnp.zeros_like(s_ref)

    q = p_ref[:, 0:256]
    z = p_ref[:, 256:512]
    v = p_ref[:, 512:768]
    g = p_ref[:, 768:1024]
    lb = lb_ref[...]
    lf = jnp.log(lb + (1.0 - lb) * _sigmoid(z))
    kx = (1.0 - lb) * _sigmoid(-z)
    qs = q * _sigmoid(q)
    bc = _cumsum_rows(lf)
    b_last = bc[C - 1:C, :]
    masks = _head_masks()
    G = _head_sum_matrix()
    S = s_ref[...]

    o = _mm_nt(qs * jnp.exp(bc), S)
    o = o + _mm_ones(qs * kx, G) * v

    ti = lax.broadcasted_iota(jnp.int32, (C, C), 0)
    si = lax.broadcasted_iota(jnp.int32, (C, C), 1)
    A = [jnp.zeros((C, C), F32) for _ in range(H_GRP)]
    bs = C
    while bs >= 2:
        hs = bs // 2
        sel = (si == (ti // bs) * bs + hs - 1).astype(F32)
        bcm = _select_rows(sel, bc)
        qt = qs * jnp.exp(jnp.minimum(bc - bcm, 0.0))
        kt = kx * jnp.exp(jnp.minimum(bcm - bc, 0.0))
        lvl = ((ti // bs) == (si // bs)) & ((ti % bs) >= hs) & ((si % bs) < hs)
        for h in range(H_GRP):
            A[h] = A[h] + jnp.where(lvl, _mm_nt(qt * masks[h], kt), 0.0)
        bs = hs
    for h in range(H_GRP):
        o = o + _mm(A[h], v) * masks[h]

    upd = _mm_tn(v, kx * jnp.exp(b_last - bc))
    ri = lax.broadcasted_iota(jnp.int32, (C_GRP, C_GRP), 0) // HEAD_DIM
    ci = lax.broadcasted_iota(jnp.int32, (C_GRP, C_GRP), 1) // HEAD_DIM
    s_ref[...] = S * jnp.exp(b_last) + jnp.where(ri == ci, upd, 0.0)

    ms = _mm_ones(o * o, G) * (1.0 / HEAD_DIM)
    o_ref[...] = o * lax.rsqrt(ms + RMS_EPS) * ng_ref[...] * (g * _sigmoid(g))


def _hgrn_call(p, lb, ng, B, Lp):
    C = SCAN_CHUNK
    nC = Lp // C
    vec = pl.BlockSpec((1, C_GRP), lambda b, c: (0, 0))
    return pl.pallas_call(
        _hgrn_kernel,
        grid=(B, nC),
        in_specs=[pl.BlockSpec((C, W_HGRN), lambda b, c: (b * nC + c, 0)), vec, vec],
        out_specs=pl.BlockSpec((C, C_GRP), lambda b, c: (b * nC + c, 0)),
        out_shape=jax.ShapeDtypeStruct((B * Lp, C_GRP), F32),
        scratch_shapes=[pltpu.VMEM((C_GRP, C_GRP), F32)],
        compiler_params=_cparams(("parallel", "arbitrary")),
        name="hgrn2",
    )(p, lb, ng)


def _rope_tables(Lp, width, head, rd):
    half = rd // 2
    pos = np.arange(Lp, dtype=np.float32)
    inv_freq = jnp.asarray(ROPE_THETA, F32) ** (-jnp.arange(half, dtype=F32) / half)
    ang = jnp.asarray(pos)[:, None] * inv_freq[None, :]
    cos, sin = jnp.cos(ang), jnp.sin(ang)
    d = np.arange(width) % head
    fi = np.where(d < half, d, d - half) % half
    is1 = jnp.asarray(d < half)
    is2 = jnp.asarray((d >= half) & (d < rd))
    cosl, sinl = cos[:, fi], sin[:, fi]
    ct = jnp.where(is1 | is2, cosl, 1.0)
    s1 = jnp.where(is1, -sinl, 0.0)
    s2 = jnp.where(is2, sinl, 0.0)
    return ct, s1, s2


def _rope(x, ct, s1, s2, half):
    n = x.shape[1]
    return x * ct + pltpu.roll(x, n - half, axis=1) * s1 + pltpu.roll(x, half, axis=1) * s2


def _dsa_prep_kernel(p_ref, cq_ref, s1q_ref, s2q_ref, ci_ref, s1i_ref, s2i_ref,
                     ck_ref, s1k_ref, s2k_ref, eq_ref, ek_ref,
                     q_out, k_out, v_out, qs_out, ks_out, kw_out):
    cq, s1q, s2q = cq_ref[...], s1q_ref[...], s2q_ref[...]
    q = _rope(p_ref[:, 0:256], cq, s1q, s2q, 8)
    q_out[...] = (q * (HEAD_DIM ** -0.5 * math.log2(math.e))).astype(BF16)
    k_out[...] = _rope(p_ref[:, 256:512], cq, s1q, s2q, 8).astype(BF16)
    v_out[...] = p_ref[:, 512:768].astype(BF16)
    qi = _rope(p_ref[:, 768:1024], ci_ref[...], s1i_ref[...], s2i_ref[...], 4)
    kw = _rope(p_ref[:, 1024:1152], ck_ref[...], s1k_ref[...], s2k_ref[...], 4)
    kw_out[...] = kw
    q1, q2, q3 = _split3(qi)
    qs = jnp.dot(q1, eq_ref[0], preferred_element_type=F32)
    qs += jnp.dot(q2, eq_ref[1], preferred_element_type=F32)
    qs += jnp.dot(q3, eq_ref[2], preferred_element_type=F32)
    qs_out[...] = qs.astype(BF16)
    k1, k2, k3 = _split3(kw)
    ks = jnp.dot(k1, ek_ref[0], preferred_element_type=F32)
    ks += jnp.dot(k2, ek_ref[1], preferred_element_type=F32)
    ks += jnp.dot(k3, ek_ref[2], preferred_element_type=F32)
    ks_out[...] = ks.astype(BF16)


def _split_expanders():
    q_slots = ((0, 1, 3), (2, 4), (5,))
    k_slots = ((0, 2, 5), (1, 4), (3,))
    eq = np.zeros((3, IDX_HEADS * IDX_DIM, IDX_HEADS * 256), np.float32)
    ek = np.zeros((3, 128, 256), np.float32)
    for part in range(3):
        for j in q_slots[part]:
            for h in range(IDX_HEADS):
                for d in range(IDX_DIM):
                    eq[part, h * IDX_DIM + d, h * 256 + j * IDX_DIM + d] = 1.0
        for j in k_slots[part]:
            for d in range(IDX_DIM):
                ek[part, d, j * IDX_DIM + d] = 1.0
    return jnp.asarray(eq, BF16), jnp.asarray(ek, BF16)


def _dsa_prep_call(p, B, Lp, tm):
    T = B * Lp
    nL = Lp // tm
    tq = _rope_tables(Lp, 256, HEAD_DIM, HEAD_DIM // 4)
    ti = _rope_tables(Lp, 256, IDX_DIM, IDX_DIM // 4)
    tk = tuple(jnp.concatenate([t, jnp.full((Lp, 96), fill, F32)], axis=1)
               for t, fill in zip(_rope_tables(Lp, 32, IDX_DIM, IDX_DIM // 4), (1.0, 0.0, 0.0)))
    eq, ek = _split_expanders()
    tab = lambda n: pl.BlockSpec((tm, n), lambda i: (i % nL, 0))
    row = lambda n: pl.BlockSpec((tm, n), lambda i: (i, 0))
    outs = [(256, BF16), (256, BF16), (256, BF16), (IDX_HEADS * 256, BF16), (256, BF16), (128, F32)]
    return pl.pallas_call(
        _dsa_prep_kernel,
        grid=(T // tm,),
        in_specs=[row(W_DSA)] + [tab(256)] * 6 + [tab(128)] * 3
        + [pl.BlockSpec(eq.shape, lambda i: (0, 0, 0)), pl.BlockSpec(ek.shape, lambda i: (0, 0, 0))],
        out_specs=[row(n) for n, _ in outs],
        out_shape=[jax.ShapeDtypeStruct((T, n), dt) for n, dt in outs],
        compiler_params=_cparams(("parallel",)),
        name="dsa_prep",
    )(p, *tq, *ti, *tk, eq, ek)


def _dsa_kernel(q_ref, qs_ref, w_ref, k_ref, ks_ref, vt_ref, o_ref,
                key_ref, k16_ref, acc_ref, s_ref, p_ref, *, topk):
    TQ, KB = DSA_TQ, DSA_KB
    i = pl.program_id(1)
    q0 = i * TQ
    nkb = (q0 + TQ + KB - 1) // KB
    qpos = q0 + lax.broadcasted_iota(jnp.int32, (1, TQ), 1)
    krow = lax.broadcasted_iota(jnp.int32, (KB, 1), 0)
    nt = (((1,), (1,)), ((), ()))

    def kslice(j):
        return pl.ds(pl.multiple_of(j * KB, KB), KB)

    w = w_ref[...] * (IDX_HEADS ** -0.5 * IDX_DIM ** -0.5)

    def idx_body(j, carry):
        ks = ks_ref[kslice(j), :]
        acc = jnp.zeros((KB, TQ), F32)
        for h in range(IDX_HEADS):
            s = lax.dot_general(ks, qs_ref[:, h * 256:(h + 1) * 256], nt, preferred_element_type=F32)
            acc = acc + w[h:h + 1, :] * jnp.maximum(s, 0.0)
        acc = jnp.where(j * KB + krow <= qpos, acc, NEG_INF)
        bits = pltpu.bitcast(acc, jnp.int32)
        key = bits ^ ((bits >> 31) & jnp.int32(0x7FFFFFFF))
        key_ref[kslice(j), :] = key
        k16_ref[kslice(j), :] = (key >> 16).astype(jnp.int16)
        return carry

    lax.fori_loop(0, nkb, idx_body, 0)

    def count_ge(cand):
        def body(j, cnt):
            hit = (key_ref[kslice(j), :] >= cand).astype(jnp.int32)
            return cnt + jnp.sum(hit.reshape(KB // 8, 8, TQ), axis=0)
        cnt = lax.fori_loop(0, nkb, body, jnp.zeros((8, TQ), jnp.int32))
        return jnp.sum(cnt, axis=0, keepdims=True)

    def count_ge16(cand):
        c16 = cand.astype(jnp.int16)
        def body(j, cnt):
            hit = (k16_ref[kslice(j), :] >= c16).astype(jnp.int16)
            parts = [hit[r:r + 16] for r in range(0, KB, 16)]
            while len(parts) > 1:
                parts = [a + b for a, b in zip(parts[::2], parts[1::2])]
            return cnt + parts[0]
        cnt = lax.fori_loop(0, nkb, body, jnp.zeros((16, TQ), jnp.int16))
        return jnp.sum(cnt.astype(jnp.int32), axis=0, keepdims=True)

    def select16(want):
        def bit_body(b, res_u):
            cand_u = res_u | (jnp.int32(1) << (15 - b))
            cnt = count_ge16(cand_u - HALF)
            return jnp.where(cnt >= want, cand_u, res_u)
        return lax.fori_loop(0, 16, bit_body, jnp.zeros((1, TQ), jnp.int32))

    HALF = 32768
    hi = select16(topk) - HALF
    cnt_above = jnp.where(hi == HALF - 1, 0, count_ge16(jnp.minimum(hi + 1, HALF - 1)))
    hi16 = hi.astype(jnp.int16)

    def low_body(j, carry):
        lo = ((key_ref[kslice(j), :] & jnp.int32(0xFFFF)) - HALF).astype(jnp.int16)
        k16_ref[kslice(j), :] = jnp.where(k16_ref[kslice(j), :] == hi16, lo, jnp.int16(-HALF))
        return carry

    lax.fori_loop(0, nkb, low_body, 0)
    thr = (hi << 16) | select16(topk - cnt_above)
    cnt_ge = count_ge(thr)
    cnt_gt = count_ge(thr + 1)
    n_take = topk - cnt_gt
    neg_key = jnp.int32(np.array(NEG_INF, np.float32).view(np.int32)) ^ jnp.int32(0x7FFFFFFF)
    need_rank = jnp.any(((cnt_ge - cnt_gt) > n_take) & (thr > neg_key))

    def put_bias(j, sel):
        key_ref[kslice(j), :] = pltpu.bitcast(jnp.where(sel, 0.0, NEG_INF), jnp.int32)

    def fast_body(j, carry):
        put_bias(j, (key_ref[kslice(j), :] >= thr) & (j * KB + krow <= qpos))
        return carry

    def rank_body(j, seen):
        blk = key_ref[kslice(j), :]
        eq = blk == thr
        ri = lax.broadcasted_iota(jnp.int32, (KB, KB), 0)
        ci = lax.broadcasted_iota(jnp.int32, (KB, KB), 1)
        lower = (ci <= ri).astype(BF16)
        rank = seen + jnp.dot(lower, eq.astype(BF16), preferred_element_type=F32)
        put_bias(j, ((blk > thr) | (eq & (rank <= n_take.astype(F32)))) & (j * KB + krow <= qpos))
        return seen + jnp.sum(eq.astype(F32), axis=0, keepdims=True)

    @pl.when(jnp.logical_not(need_rank))
    def _():
        lax.fori_loop(0, nkb, fast_body, 0)

    @pl.when(need_rank)
    def _():
        lax.fori_loop(0, nkb, rank_body, jnp.zeros((1, TQ), F32))

    lane_head = lax.broadcasted_iota(jnp.int32, (1, C_GRP), 1) // HEAD_DIM
    q = q_ref[...]
    qh = [jnp.where(lane_head == h, q, jnp.zeros_like(q)) for h in range(H_GRP)]

    def scores(j, h):
        bias = pltpu.bitcast(key_ref[kslice(j), :], F32)
        return lax.dot_general(k_ref[kslice(j), :], qh[h], nt, preferred_element_type=F32) + bias

    def max_body(j, ms):
        return tuple(jnp.maximum(m, jnp.max(scores(j, h).reshape(KB // 8, 8, TQ), axis=0))
                     for h, m in enumerate(ms))

    ms = lax.fori_loop(0, nkb, max_body, tuple(jnp.full((8, TQ), NEG_INF, F32) for _ in range(H_GRP)))
    ms = [jnp.maximum(jnp.max(m, axis=0, keepdims=True), 0.1 * NEG_INF) for m in ms]

    acc_ref[...] = jnp.zeros(acc_ref.shape, F32)

    def pv_body(j, ls):
        for h in range(H_GRP):
            s_ref[h] = scores(j, h)
        out = []
        for h in range(H_GRP):
            pexp = jnp.exp2(s_ref[h] - ms[h])
            out.append(ls[h] + jnp.sum(pexp.reshape(KB // 8, 8, TQ), axis=0))
            p_ref[h] = pexp.astype(BF16)
        for h in range(H_GRP):
            hs = slice(h * HEAD_DIM, (h + 1) * HEAD_DIM)
            acc_ref[hs, :] += jnp.dot(vt_ref[hs, kslice(j)], p_ref[h], preferred_element_type=F32)
        return tuple(out)

    ls = lax.fori_loop(0, nkb, pv_body, tuple(jnp.zeros((8, TQ), F32) for _ in range(H_GRP)))
    for h in range(H_GRP):
        hs = slice(h * HEAD_DIM, (h + 1) * HEAD_DIM)
        acc_ref[hs, :] = acc_ref[hs, :] / jnp.sum(ls[h], axis=0, keepdims=True)
    o_ref[...] = acc_ref[...].T


def _dsa_call(q, qs, wt, k3, ks3, vt3, B, Lp, topk):
    TQ = DSA_TQ
    nQ = Lp // TQ
    rowq = lambda n: pl.BlockSpec((TQ, n), lambda b, i: (b * nQ + i, 0))
    full = lambda r, c: pl.BlockSpec((None, r, c), lambda b, i: (b, 0, 0))
    return pl.pallas_call(
        functools.partial(_dsa_kernel, topk=topk),
        grid=(B, nQ),
        in_specs=[rowq(256), rowq(IDX_HEADS * 256),
                  pl.BlockSpec((None, IDX_HEADS, TQ), lambda b, i: (b, 0, i)),
                  full(Lp, 256), full(Lp, 256), full(256, Lp)],
        out_specs=rowq(C_GRP),
        out_shape=jax.ShapeDtypeStruct((B * Lp, C_GRP), F32),
        scratch_shapes=[pltpu.VMEM((Lp, TQ), jnp.int32),
                        pltpu.VMEM((Lp, TQ), jnp.int16),
                        pltpu.VMEM((C_GRP, TQ), F32),
                        pltpu.VMEM((H_GRP, DSA_KB, TQ), F32),
                        pltpu.VMEM((H_GRP, DSA_KB, TQ), BF16)],
        compiler_params=_cparams(("parallel", "arbitrary")),
        name="dsa_attn",
    )(q, qs, wt, k3, ks3, vt3)


MOE_TILE = 2048
MOE_SUB = 512
MOE_GRP = 128
MOE_ALIGN = 16


def _moe_kernel(x_ref, gt_ref, w1_ref, b1_ref, w2_ref, b2_ref, o_ref,
                rank_ref, xc_ref, yc_ref, gc_ref):
    e = pl.program_id(1)
    Tt = x_ref.shape[0]
    NS = Tt // MOE_SUB
    GRP = MOE_GRP

    @pl.when(jnp.logical_and(pl.program_id(0) == 0, e == 0))
    def _():
        xc_ref[...] = jnp.zeros_like(xc_ref)
        yc_ref[...] = jnp.zeros_like(yc_ref)
        gc_ref[...] = jnp.zeros_like(gc_ref)

    @pl.when(e == 0)
    def _():
        o_ref[...] = jnp.zeros_like(o_ref)
        ri = lax.broadcasted_iota(jnp.int32, (MOE_SUB, MOE_SUB), 0)
        ci = lax.broadcasted_iota(jnp.int32, (MOE_SUB, MOE_SUB), 1)
        upper = (ri <= ci).astype(BF16)
        for s in range(NS):
            hot = (gt_ref[:, s * MOE_SUB:(s + 1) * MOE_SUB] > 0.0).astype(BF16)
            rank_ref[:, s * MOE_SUB:(s + 1) * MOE_SUB] = jnp.dot(hot, upper, preferred_element_type=F32)

    riota = lax.broadcasted_iota(jnp.int32, (GRP, 1), 0).astype(F32)

    def onehot(s, g):
        cols = slice(s * MOE_SUB, (s + 1) * MOE_SUB)
        gate = gt_ref[pl.ds(e, 1), cols]
        rank = rank_ref[pl.ds(e, 1), cols]
        pick = (rank == riota + (g * GRP + 1).astype(F32)) & (gate > 0.0)
        return pick, gate

    def count(s):
        cols = slice(s * MOE_SUB, (s + 1) * MOE_SUB)
        return jnp.max(rank_ref[pl.ds(e, 1), cols]).astype(jnp.int32)

    off = jnp.int32(0)
    offs = []
    for s in range(NS):
        n = count(s)
        offs.append(off)

        def gather(g, carry, s=s, off=off):
            pick, gate = onehot(s, g)
            rows = jnp.dot(pick.astype(BF16), x_ref[s * MOE_SUB:(s + 1) * MOE_SUB, :],
                           preferred_element_type=F32)
            dst = pl.ds(pl.multiple_of(off + g * GRP, MOE_ALIGN), GRP)
            xc_ref[dst, :] = rows.astype(BF16)
            gc_ref[dst, :] = jnp.broadcast_to(
                jnp.sum(jnp.where(pick, gate, 0.0), axis=1, keepdims=True), (GRP, 128))
            return carry

        lax.fori_loop(0, (n + GRP - 1) // GRP, gather, 0)
        off = off + ((n + MOE_ALIGN - 1) // MOE_ALIGN) * MOE_ALIGN

    DE = D_MODEL

    def ffn(c, carry):
        rows = pl.ds(pl.multiple_of(c * GRP, GRP), GRP)
        u = jnp.dot(xc_ref[rows, :], w1_ref[0], preferred_element_type=F32) + b1_ref[0]
        glu = jnp.minimum(u[:, :DE], SWIGLU_LIMIT)
        lin = jnp.clip(u[:, DE:], -SWIGLU_LIMIT, SWIGLU_LIMIT)
        act = glu * _sigmoid(SWIGLU_ALPHA * glu) * (lin + 1.0)
        y = jnp.dot(act.astype(BF16), w2_ref[0], preferred_element_type=F32) + b2_ref[0]
        yc_ref[rows, :] = (y * gc_ref[rows, 0:1]).astype(BF16)
        return carry

    lax.fori_loop(0, (off + GRP - 1) // GRP, ffn, 0)

    for s in range(NS):
        n = count(s)

        def scatter(g, carry, s=s, off=offs[s]):
            pick, _ = onehot(s, g)
            src = pl.ds(pl.multiple_of(off + g * GRP, MOE_ALIGN), GRP)
            o_ref[s * MOE_SUB:(s + 1) * MOE_SUB, :] += lax.dot_general(
                pick.astype(BF16), yc_ref[src, :], (((0,), (0,)), ((), ())), preferred_element_type=F32)
            return carry

        lax.fori_loop(0, (n + GRP - 1) // GRP, scatter, 0)


def _moe_call(xb, gt, w1, b1, w2, b2, Tt):
    T, D = xb.shape
    E = N_EXPERTS
    cap = Tt + (Tt // MOE_SUB) * MOE_ALIGN + 2 * MOE_GRP
    return pl.pallas_call(
        _moe_kernel,
        grid=(T // Tt, E),
        in_specs=[pl.BlockSpec((Tt, D), lambda i, e: (i, 0)),
                  pl.BlockSpec((E, Tt), lambda i, e: (0, i)),
                  pl.BlockSpec((1, D, 2 * D), lambda i, e: (e, 0, 0)),
                  pl.BlockSpec((1, 1, 2 * D), lambda i, e: (e, 0, 0)),
                  pl.BlockSpec((1, D, D), lambda i, e: (e, 0, 0)),
                  pl.BlockSpec((1, 1, D), lambda i, e: (e, 0, 0))],
        out_specs=pl.BlockSpec((Tt, D), lambda i, e: (i, 0)),
        out_shape=jax.ShapeDtypeStruct((T, D), F32),
        scratch_shapes=[pltpu.VMEM((E, Tt), F32), pltpu.VMEM((cap, D), BF16),
                        pltpu.VMEM((cap, D), BF16), pltpu.VMEM((cap, 128), F32)],
        compiler_params=_cparams(("arbitrary", "arbitrary")),
        name="moe",
    )(xb, gt, w1, b1.reshape(E, 1, 2 * D), w2, b2.reshape(E, 1, D))


def _ln_res_kernel(h_ref, f_ref, g_ref, b_ref, o_ref):
    o_ref[...] = _layernorm_rows(DEEPNORM_ALPHA * h_ref[...] + f_ref[...], g_ref[...], b_ref[...])


def _ln_res_call(h, f, g, b, tm):
    T, D = h.shape
    return pl.pallas_call(
        _ln_res_kernel,
        grid=(T // tm,),
        in_specs=[pl.BlockSpec((tm, D), lambda i: (i, 0)),
                  pl.BlockSpec((tm, D), lambda i: (i, 0)),
                  pl.BlockSpec((1, D), lambda i: (0, 0)),
                  pl.BlockSpec((1, D), lambda i: (0, 0))],
        out_specs=pl.BlockSpec((tm, D), lambda i: (i, 0)),
        out_shape=jax.ShapeDtypeStruct((T, D), F32),
        compiler_params=_cparams(("parallel",)),
        name="ln_ffn",
    )(h, f, g.reshape(1, D), b.reshape(1, D))


def _regroup_w_in(w):
    D = w.shape[0]
    z = lambda n: jnp.zeros((D, n), w.dtype)
    o = 0
    r, wd, k, v, ad, gd = (w[:, a:b] for a, b in _bounds(o, (256, 64, 256, 256, 64, 64)))
    o += 960
    dsa = w[:, o:o + 1064]
    o += 1064
    lru = w[:, o:o + 512]
    o += 512
    hg = w[:, o:o + 1024]
    return jnp.concatenate([r, k, v, wd, ad, gd, z(64), dsa, z(W_DSA - 1064), lru, hg], axis=1).astype(BF16)


def _bounds(start, sizes):
    out = []
    for s in sizes:
        out.append((start, start + s))
        start += s
    return out


def _regroup_mu(mu):
    r, wd, k, v, ad, gd = (mu[a:b] for a, b in _bounds(0, (256, 64, 256, 256, 64, 64)))
    return jnp.concatenate([r, k, v, wd, ad, gd, jnp.zeros((64,), mu.dtype)]).reshape(1, W_RWKV)


def _lora_pad(w, slot):
    return jnp.zeros((C_GRP, C_GRP), w.dtype).at[slot * LORA:(slot + 1) * LORA].set(w).astype(BF16)


def _block_diag(w):
    out = jnp.zeros((C_GRP, C_GRP), w.dtype)
    for h in range(H_GRP):
        out = out.at[h * HEAD_DIM:(h + 1) * HEAD_DIM, h * HEAD_DIM:(h + 1) * HEAD_DIM].set(w[h])
    return out.astype(BF16)


def _pick_moe_tile(T):
    for tt in (MOE_TILE, 1024, MOE_SUB):
        if T % tt == 0:
            return tt
    raise ValueError(f"no MoE tile for {T}")


def _pick_tile(T, pref):
    for tm in (pref, 512, 256, 128, 64, 32, 16, 8):
        if tm <= pref and T % tm == 0:
            return tm
    raise ValueError(f"no row tile for {T}")


def kernel(x, meta_tokens, ln_in_g, ln_in_b, w_in, rwkv_mu, rwkv_w0, rwkv_w2, rwkv_a0, rwkv_a2,
           rwkv_g2, rwkv_k_k, rwkv_k_a, rwkv_r_k, rwkv_ln_g, rwkv_ln_b, lru_conv_w, lru_conv_b,
           lru_wa, lru_ba, lru_wx, lru_bx, lru_lambda, hgrn_lower_bounds, hgrn_norm_g, w_out,
           ln_mix_g, ln_mix_b, router_w, router_b, moe_w1, moe_b1, moe_w2, moe_b2,
           ln_ffn_g, ln_ffn_b):
    B, S, D = x.shape
    assert D == D_MODEL
    depth = w_in.shape[0]
    topk = min(TOPK_MAX, S // 4)
    L = S + N_META
    Lp = -(-L // SEQ_ALIGN) * SEQ_ALIGN
    T = B * Lp
    tm = _pick_tile(T, 512)
    v1 = lambda t: t.reshape(1, -1)

    h = jnp.concatenate([jnp.broadcast_to(meta_tokens[None].astype(x.dtype), (B, N_META, D)), x,
                         jnp.zeros((B, Lp - L, D), x.dtype)], axis=1).reshape(T, D)
    h = _ln_call(h, ln_in_g, ln_in_b, tm)

    s_lb = jax.nn.softmax(hgrn_lower_bounds.astype(F32), axis=0)
    lower_bounds = jnp.cumsum(s_lb, axis=0) - s_lb[0]

    for l in range(depth):
        p_rwkv, p_dsa, p_lru, p_hgrn = _proj_call(h, _regroup_w_in(w_in[l]), tm)

        y_rwkv = _rwkv_call(p_rwkv, (
            _regroup_mu(rwkv_mu[l]), v1(rwkv_w0[l]), _lora_pad(rwkv_w2[l], 0), v1(rwkv_a0[l]),
            _lora_pad(rwkv_a2[l], 1), _lora_pad(rwkv_g2[l], 2), v1(rwkv_k_k[l]), v1(rwkv_k_a[l]),
            v1(rwkv_r_k[l]), v1(rwkv_ln_g[l]), v1(rwkv_ln_b[l])), B, Lp)

        q, k, v, qs, ks, kw = _dsa_prep_call(p_dsa, B, Lp, _pick_tile(Lp, 512))
        wt = jnp.swapaxes(kw.reshape(B, Lp, 128)[:, :, IDX_DIM:IDX_DIM + IDX_HEADS], 1, 2)
        vt = jnp.swapaxes(v.reshape(B, Lp, 256), 1, 2)
        y_dsa = _dsa_call(q, qs, wt, k.reshape(B, Lp, 256), ks.reshape(B, Lp, 256), vt, B, Lp, topk)

        y_lru = _lru_call(p_lru, (lru_conv_w[l], v1(lru_conv_b[l]), _block_diag(lru_wa[l]),
                                  v1(lru_ba[l]), _block_diag(lru_wx[l]), v1(lru_bx[l]),
                                  v1(lru_lambda[l])), B, Lp)

        y_hgrn = _hgrn_call(p_hgrn, v1(lower_bounds[l]), v1(hgrn_norm_g[l]), B, Lp)

        h, hb, gates_t = _mix_call(h, (y_rwkv, y_dsa, y_lru, y_hgrn), w_out[l].astype(BF16),
                                   ln_mix_g[l], ln_mix_b[l], router_w[l], router_b[l], tm)
        ffn = _moe_call(hb, gates_t, moe_w1[l].astype(BF16), moe_b1[l], moe_w2[l].astype(BF16),
                        moe_b2[l], _pick_moe_tile(T))
        h = _ln_res_call(h, ffn, ln_ffn_g[l], ln_ffn_b[l], tm)

    return h.reshape(B, Lp, D)[:, N_META:L]
```

```python
import functools
import math

import numpy as np
import jax
import jax.numpy as jnp
from jax import lax
from jax.experimental import pallas as pl
from jax.experimental.pallas import tpu as pltpu

F32 = jnp.float32
BF16 = jnp.bfloat16
HIGHEST = lax.Precision.HIGHEST

D_MODEL = 1024
N_META = 16
HEAD_DIM = 64
C_GRP = 256
H_GRP = 4
LORA = 64
IDX_HEADS = 8
IDX_DIM = 32
TOPK_MAX = 256
ROPE_THETA = 500000.0
LRU_C = 8.0
CONV_WIDTH = 4
LN_EPS = 1e-5
RMS_EPS = 1e-5
RWKV_LN_EPS = 64e-5
NEG_INF = -1e30
N_EXPERTS = 32
TOP_K = 4
SWIGLU_ALPHA = 1.702
SWIGLU_LIMIT = 7.0
DEPTH = 2
DEEPNORM_ALPHA = (2 * DEPTH) ** 0.25

SEQ_ALIGN = 256
RWKV_CHUNK = 64
RWKV_ROWS = 2
SCAN_CHUNK = 128
DSA_TQ = 256
DSA_KB = 256
VMEM_LIMIT = 56 * 1024 * 1024

W_RWKV = 1024
W_DSA = 1152
W_LRU = 512
W_HGRN = 1024


def _cparams(sem):
    return pltpu.CompilerParams(dimension_semantics=sem, vmem_limit_bytes=VMEM_LIMIT)


def _mm(a, b):
    return jnp.dot(a.astype(BF16), b.astype(BF16), preferred_element_type=F32)


def _mm_nt(a, b):
    return lax.dot_general(a.astype(BF16), b.astype(BF16), (((1,), (1,)), ((), ())),
                           preferred_element_type=F32)


def _mm_tn(a, b):
    return lax.dot_general(a.astype(BF16), b.astype(BF16), (((0,), (0,)), ((), ())),
                           preferred_element_type=F32)


def _mmh(a, b):
    return jnp.dot(a, b, precision=HIGHEST, preferred_element_type=F32)


def _split2(x):
    hi = x.astype(BF16)
    return hi, (x - hi.astype(F32)).astype(BF16)


def _split3(x):
    x1 = x.astype(BF16)
    r1 = x - x1.astype(F32)
    x2 = r1.astype(BF16)
    x3 = (r1 - x2.astype(F32)).astype(BF16)
    return x1, x2, x3


def _mm3(a, b):
    ah, al = _split2(a)
    bh, bl = _split2(b)
    d = lambda x, y: jnp.dot(x, y, preferred_element_type=F32)
    return d(ah, bh) + (d(ah, bl) + d(al, bh))


def _mm_ones(a, g):
    hi, lo = _split2(a)
    gb = g.astype(BF16)
    return (jnp.dot(hi, gb, preferred_element_type=F32) + jnp.dot(lo, gb, preferred_element_type=F32))


def _select_rows(sel, x):
    x1, x2, x3 = _split3(x)
    sb = sel.astype(BF16)
    return (jnp.dot(sb, x1, preferred_element_type=F32) + jnp.dot(sb, x2, preferred_element_type=F32)
            + jnp.dot(sb, x3, preferred_element_type=F32))


def _sigmoid(x):
    return 1.0 / (1.0 + jnp.exp(-x))


def _softplus(x):
    return jnp.maximum(x, 0.0) + jnp.log1p(jnp.exp(-jnp.abs(x)))


def _head_masks():
    lane = lax.broadcasted_iota(jnp.int32, (1, C_GRP), 1) // HEAD_DIM
    return [(lane == h).astype(F32) for h in range(H_GRP)]


def _head_sum_matrix():
    r = lax.broadcasted_iota(jnp.int32, (C_GRP, C_GRP), 0) // HEAD_DIM
    c = lax.broadcasted_iota(jnp.int32, (C_GRP, C_GRP), 1) // HEAD_DIM
    return (r == c).astype(F32)


def _cumsum_rows(x):
    n = x.shape[0]
    row = lax.broadcasted_iota(jnp.int32, (n, 1), 0)
    s = 1
    while s < n:
        x = x + jnp.where(row >= s, pltpu.roll(x, s, axis=0), 0.0)
        s *= 2
    return x


def _layernorm_rows(x, g, b):
    mu = jnp.mean(x, axis=-1, keepdims=True)
    xc = x - mu
    var = jnp.mean(xc * xc, axis=-1, keepdims=True)
    return xc * lax.rsqrt(var + LN_EPS) * g + b


def _ln_kernel(x_ref, g_ref, b_ref, o_ref):
    o_ref[...] = _layernorm_rows(x_ref[...], g_ref[...], b_ref[...])


def _ln_call(x, g, b, tm):
    T, D = x.shape
    return pl.pallas_call(
        _ln_kernel,
        grid=(T // tm,),
        in_specs=[pl.BlockSpec((tm, D), lambda i: (i, 0)),
                  pl.BlockSpec((1, D), lambda i: (0, 0)),
                  pl.BlockSpec((1, D), lambda i: (0, 0))],
        out_specs=pl.BlockSpec((tm, D), lambda i: (i, 0)),
        out_shape=jax.ShapeDtypeStruct((T, D), F32),
        compiler_params=_cparams(("parallel",)),
        name="ln_in",
    )(x, g.reshape(1, D), b.reshape(1, D))


def _proj_kernel(h_ref, w_ref, o_rwkv, o_dsa, o_lru, o_hgrn):
    hb = h_ref[...].astype(BF16)
    off = 0
    for o, n in ((o_rwkv, W_RWKV), (o_dsa, W_DSA), (o_lru, W_LRU), (o_hgrn, W_HGRN)):
        o[...] = jnp.dot(hb, w_ref[:, off:off + n], preferred_element_type=F32)
        off += n


def _proj_call(h, w, tm):
    T, D = h.shape
    widths = (W_RWKV, W_DSA, W_LRU, W_HGRN)
    return pl.pallas_call(
        _proj_kernel,
        grid=(T // tm,),
        in_specs=[pl.BlockSpec((tm, D), lambda i: (i, 0)),
                  pl.BlockSpec((D, sum(widths)), lambda i: (0, 0))],
        out_specs=[pl.BlockSpec((tm, n), lambda i: (i, 0)) for n in widths],
        out_shape=[jax.ShapeDtypeStruct((T, n), F32) for n in widths],
        compiler_params=_cparams(("parallel",)),
        name="proj_in",
    )(h, w)


def _mix_kernel(h_ref, y0, y1, y2, y3, w_ref, g_ref, b_ref, rwt_ref, rb_ref, o_ref, ob_ref, gt_ref):
    mix = _mm(y0[...], w_ref[0:C_GRP, :])
    mix += _mm(y1[...], w_ref[C_GRP:2 * C_GRP, :])
    mix += _mm(y2[...], w_ref[2 * C_GRP:3 * C_GRP, :])
    mix += _mm(y3[...], w_ref[3 * C_GRP:4 * C_GRP, :])
    h = _layernorm_rows(DEEPNORM_ALPHA * h_ref[...] + mix, g_ref[...], b_ref[...])
    o_ref[...] = h
    ob_ref[...] = h.astype(BF16)

    tm = h.shape[0]
    logits = lax.dot_general(rwt_ref[...], h, (((1,), (1,)), ((), ())), precision=HIGHEST,
                             preferred_element_type=F32) + rb_ref[...]
    row = lax.broadcasted_iota(jnp.int32, (N_EXPERTS, tm), 0)
    work = logits
    vals, hots = [], []
    for _ in range(TOP_K):
        mx = jnp.max(work, axis=0, keepdims=True)
        first = jnp.min(jnp.where(work == mx, row, N_EXPERTS), axis=0, keepdims=True)
        hot = row == first
        vals.append(mx)
        hots.append(hot)
        work = jnp.where(hot, -jnp.inf, work)
    ex = [jnp.exp(vv - vals[0]) for vv in vals]
    den = ex[0] + ex[1] + ex[2] + ex[3]
    gates = jnp.zeros((N_EXPERTS, tm), F32)
    for hot, w in zip(hots, ex):
        gates = gates + jnp.where(hot, w / den, 0.0)
    gt_ref[...] = gates


def _mix_call(h, ys, w, g, b, rw, rb, tm):
    T, D = h.shape
    E = N_EXPERTS
    return pl.pallas_call(
        _mix_kernel,
        grid=(T // tm,),
        in_specs=[pl.BlockSpec((tm, D), lambda i: (i, 0))]
        + [pl.BlockSpec((tm, C_GRP), lambda i: (i, 0))] * 4
        + [pl.BlockSpec((4 * C_GRP, D), lambda i: (0, 0)),
           pl.BlockSpec((1, D), lambda i: (0, 0)),
           pl.BlockSpec((1, D), lambda i: (0, 0)),
           pl.BlockSpec((E, D), lambda i: (0, 0)),
           pl.BlockSpec((E, 1), lambda i: (0, 0))],
        out_specs=[pl.BlockSpec((tm, D), lambda i: (i, 0)),
                   pl.BlockSpec((tm, D), lambda i: (i, 0)),
                   pl.BlockSpec((E, tm), lambda i: (0, i))],
        out_shape=[jax.ShapeDtypeStruct((T, D), F32), jax.ShapeDtypeStruct((T, D), BF16),
                   jax.ShapeDtypeStruct((E, T), F32)],
        compiler_params=_cparams(("parallel",)),
        name="mix_out",
    )(h, *ys, w, g.reshape(1, D), b.reshape(1, D), rw.T, rb.reshape(E, 1))


def _rwkv_kernel(p_ref, mu_ref, w0_ref, w2_ref, a0_ref, a2_ref, g2_ref, kk_ref, ka_ref, rk_ref,
                 lng_ref, lnb_ref, o_ref, prev_ref, s_ref):
    @pl.when(pl.program_id(1) == 0)
    def _():
        prev_ref[...] = jnp.zeros_like(prev_ref)
        s_ref[...] = jnp.zeros_like(s_ref)

    for bb in range(p_ref.shape[0]):
        _rwkv_chunk(p_ref.at[bb], mu_ref, w0_ref, w2_ref, a0_ref, a2_ref, g2_ref, kk_ref, ka_ref,
                    rk_ref, lng_ref, lnb_ref, o_ref.at[bb], prev_ref.at[bb], s_ref.at[bb])


def _rwkv_chunk(p_ref, mu_ref, w0_ref, w2_ref, a0_ref, a2_ref, g2_ref, kk_ref, ka_ref, rk_ref,
                lng_ref, lnb_ref, o_ref, prev_ref, s_ref):
    C = RWKV_CHUNK
    p = p_ref[...]
    row = lax.broadcasted_iota(jnp.int32, (C, 1), 0)
    p_prev = jnp.where(row == 0, prev_ref[0:1, :], pltpu.roll(p, 1, axis=0))
    prev_ref[0:1, :] = p[C - 1:C, :]
    ps = p + (p_prev - p) * mu_ref[...]
    r = ps[:, 0:256]
    k = ps[:, 256:512]
    v = ps[:, 512:768]
    lora = ps[:, 768:1024]

    w = -_softplus(-(w0_ref[...] + _mm(jnp.tanh(lora), w2_ref[...]))) - 0.5
    logw = -jnp.exp(w)
    a = _sigmoid(a0_ref[...] + _mm(lora, a2_ref[...]))
    g = _mm(_sigmoid(lora), g2_ref[...])

    G = _head_sum_matrix()
    kk = k * kk_ref[...]
    kk = kk / jnp.maximum(jnp.sqrt(_mm_ones(kk * kk, G)), 1e-12)
    k_mod = k * (1.0 + (a - 1.0) * ka_ref[...])
    a_vec = -kk
    b_vec = kk * a

    cum = _cumsum_rows(logw)
    cum_last = cum[C - 1:C, :]
    e_inv = jnp.exp(-cum)
    g_end = jnp.exp(cum_last - cum)
    masks = _head_masks()

    def stack(x):
        return jnp.concatenate([x * m for m in masks], axis=0)

    Am = stack(a_vec * jnp.exp(cum - logw))
    Rm = stack(r * jnp.exp(cum))
    Bm = stack(b_vec * e_inv)
    Km = stack(k_mod * e_inv)
    Vs = stack(v)
    BKg = jnp.concatenate([stack(b_vec * g_end), stack(k_mod * g_end)], axis=0)

    ri = lax.broadcasted_iota(jnp.int32, (4 * C, 4 * C), 0)
    ci = lax.broadcasted_iota(jnp.int32, (4 * C, 4 * C), 1)
    same = (ri // C) == (ci // C)
    strict = same & (ci < ri)
    incl = same & (ci <= ri)

    N = jnp.where(strict, _mm_nt(Am, Bm), 0.0)
    Aak = jnp.where(strict, _mm_nt(Am, Km), 0.0)
    Arb = jnp.where(incl, _mm_nt(Rm, Bm), 0.0)
    Ark = jnp.where(incl, _mm_nt(Rm, Km), 0.0)

    base = 8
    Nb = jnp.where((ri // base) == (ci // base), N, 0.0)
    P = jnp.where(ri == ci, 1.0, 0.0) + Nb
    M = Nb
    for _ in range(int(math.log2(base)) - 1):
        M = _mm3(M, M)
        P = P + _mm3(P, M)
    s = base
    while s < C:
        lower_left = ((ri // (2 * s)) == (ci // (2 * s))) & ((ri // s) % 2 == 1) & ((ci // s) % 2 == 0)
        P = P + _mm(P, _mm(jnp.where(lower_left, N, 0.0), P))
        s *= 2

    S = s_ref[...]
    U = _mm(P, _mm_nt(Am, S) + _mm(Aak, Vs))
    Y = _mm_nt(Rm, S) + _mm(Arb, U) + _mm(Ark, Vs)
    y = Y[0:C] + Y[C:2 * C] + Y[2 * C:3 * C] + Y[3 * C:4 * C]

    UV = jnp.concatenate([U, Vs], axis=0)
    s_ref[...] = S * jnp.exp(cum_last) + _mm_tn(UV, BKg)

    inv_n = 1.0 / HEAD_DIM
    y_mu = _mm_ones(y, G) * inv_n
    yc = y - y_mu
    y_var = _mm_ones(yc * yc, G) * inv_n
    yn = yc * lax.rsqrt(y_var + RWKV_LN_EPS) * lng_ref[...] + lnb_ref[...]
    bonus = _mm_ones(r * k_mod * rk_ref[...], G) * v
    o_ref[...] = (yn + bonus) * g


def _rwkv_call(p, prm, B, Lp):
    C = RWKV_CHUNK
    nC = Lp // C
    NB = RWKV_ROWS if B % RWKV_ROWS == 0 else 1
    vec = lambda n: pl.BlockSpec((1, n), lambda b, c: (0, 0))
    mat = pl.BlockSpec((C_GRP, C_GRP), lambda b, c: (0, 0))
    out = pl.pallas_call(
        _rwkv_kernel,
        grid=(B // NB, nC),
        in_specs=[pl.BlockSpec((NB, C, W_RWKV), lambda b, c: (b, c, 0)),
                  vec(W_RWKV), vec(C_GRP), mat, vec(C_GRP), mat, mat,
                  vec(C_GRP), vec(C_GRP), vec(C_GRP), vec(C_GRP), vec(C_GRP)],
        out_specs=pl.BlockSpec((NB, C, C_GRP), lambda b, c: (b, c, 0)),
        out_shape=jax.ShapeDtypeStruct((B, Lp, C_GRP), F32),
        scratch_shapes=[pltpu.VMEM((NB, 8, W_RWKV), F32), pltpu.VMEM((NB, C_GRP, C_GRP), F32)],
        compiler_params=_cparams(("parallel", "arbitrary")),
        name="rwkv7",
    )(p.reshape(B, Lp, W_RWKV), *prm)
    return out.reshape(B * Lp, C_GRP)


def _gelu_tanh(x):
    return 0.5 * x * (1.0 + jnp.tanh(math.sqrt(2.0 / math.pi) * (x + 0.044715 * x * x * x)))


def _lru_kernel(p_ref, cw_ref, cb_ref, wa_ref, ba_ref, wx_ref, bx_ref, lam_ref, o_ref,
                xbuf_ref, h_ref):
    C = SCAN_CHUNK
    c = pl.program_id(1)

    @pl.when(c == 0)
    def _():
        xbuf_ref[0:8, :] = jnp.zeros((8, C_GRP), F32)
        h_ref[...] = jnp.zeros_like(h_ref)

    xbuf_ref[8:8 + C, :] = p_ref[:, 0:C_GRP]
    gate = p_ref[:, C_GRP:2 * C_GRP]
    xc = cb_ref[...] + jnp.zeros((C, C_GRP), F32)
    for j in range(CONV_WIDTH):
        xc = xc + cw_ref[j:j + 1, :] * xbuf_ref[pl.ds(8 - (CONV_WIDTH - 1) + j, C), :]
    xbuf_ref[0:8, :] = xbuf_ref[C:C + 8, :]

    r = _sigmoid(_mm(xc, wa_ref[...]) + ba_ref[...])
    i = _sigmoid(_mm(xc, wx_ref[...]) + bx_ref[...])
    log_a = -LRU_C * r * _softplus(-lam_ref[...])
    a = jnp.exp(log_a)
    th = jnp.tanh(log_a)
    u = xc * i * jnp.sqrt(-2.0 * th / (1.0 - th))

    row = lax.broadcasted_iota(jnp.int32, (C, 1), 0)
    s = 1
    while s < C:
        keep = row >= s
        u = u + a * jnp.where(keep, pltpu.roll(u, s, axis=0), 0.0)
        a = a * jnp.where(keep, pltpu.roll(a, s, axis=0), 1.0)
        s *= 2
    h = u + a * h_ref[0:1, :]
    h_ref[0:1, :] = h[C - 1:C, :]
    o_ref[...] = h * _gelu_tanh(gate)


def _lru_call(p, prm, B, Lp):
    C = SCAN_CHUNK
    nC = Lp // C
    vec = pl.BlockSpec((1, C_GRP), lambda b, c: (0, 0))
    mat = pl.BlockSpec((C_GRP, C_GRP), lambda b, c: (0, 0))
    return pl.pallas_call(
        _lru_kernel,
        grid=(B, nC),
        in_specs=[pl.BlockSpec((C, W_LRU), lambda b, c: (b * nC + c, 0)),
                  pl.BlockSpec((CONV_WIDTH, C_GRP), lambda b, c: (0, 0)),
                  vec, mat, vec, mat, vec, vec],
        out_specs=pl.BlockSpec((C, C_GRP), lambda b, c: (b * nC + c, 0)),
        out_shape=jax.ShapeDtypeStruct((B * Lp, C_GRP), F32),
        scratch_shapes=[pltpu.VMEM((C + 8, C_GRP), F32), pltpu.VMEM((8, C_GRP), F32)],
        compiler_params=_cparams(("parallel", "arbitrary")),
        name="rglru",
    )(p, *prm)


def _hgrn_kernel(p_ref, lb_ref, ng_ref, o_ref, s_ref):
    C = SCAN_CHUNK
    c = pl.program_id(1)

    @pl.when(c == 0)
    def _():
        s_ref[...] = jnp.zeros_like(s_ref)

    q = p_ref[:, 0:256]
    z = p_ref[:, 256:512]
    v = p_ref[:, 512:768]
    g = p_ref[:, 768:1024]
    lb = lb_ref[...]
    lf = jnp.log(lb + (1.0 - lb) * _sigmoid(z))
    kx = (1.0 - lb) * _sigmoid(-z)
    qs = q * _sigmoid(q)
    bc = _cumsum_rows(lf)
    b_last = bc[C - 1:C, :]
    masks = _head_masks()
    G = _head_sum_matrix()
    S = s_ref[...]

    o = _mm_nt(qs * jnp.exp(bc), S)
    o = o + _mm_ones(qs * kx, G) * v

    ti = lax.broadcasted_iota(jnp.int32, (C, C), 0)
    si = lax.broadcasted_iota(jnp.int32, (C, C), 1)
    A = [jnp.zeros((C, C), F32) for _ in range(H_GRP)]
    bs = C
    while bs >= 2:
        hs = bs // 2
        sel = (si == (ti // bs) * bs + hs - 1).astype(F32)
        bcm = _select_rows(sel, bc)
        qt = qs * jnp.exp(jnp.minimum(bc - bcm, 0.0))
        kt = kx * jnp.exp(jnp.minimum(bcm - bc, 0.0))
        lvl = ((ti // bs) == (si // bs)) & ((ti % bs) >= hs) & ((si % bs) < hs)
        for h in range(H_GRP):
            A[h] = A[h] + jnp.where(lvl, _mm_nt(qt * masks[h], kt), 0.0)
        bs = hs
    for h in range(H_GRP):
        o = o + _mm(A[h], v) * masks[h]

    upd = _mm_tn(v, kx * jnp.exp(b_last - bc))
    ri = lax.broadcasted_iota(jnp.int32, (C_GRP, C_GRP), 0) // HEAD_DIM
    ci = lax.broadcasted_iota(jnp.int32, (C_GRP, C_GRP), 1) // HEAD_DIM
    s_ref[...] = S * jnp.exp(b_last) + jnp.where(ri == ci, upd, 0.0)

    ms = _mm_ones(o * o, G) * (1.0 / HEAD_DIM)
    o_ref[...] = o * lax.rsqrt(ms + RMS_EPS) * ng_ref[...] * (g * _sigmoid(g))


def _hgrn_call(p, lb, ng, B, Lp):
    C = SCAN_CHUNK
    nC = Lp // C
    vec = pl.BlockSpec((1, C_GRP), lambda b, c: (0, 0))
    return pl.pallas_call(
        _hgrn_kernel,
        grid=(B, nC),
        in_specs=[pl.BlockSpec((C, W_HGRN), lambda b, c: (b * nC + c, 0)), vec, vec],
        out_specs=pl.BlockSpec((C, C_GRP), lambda b, c: (b * nC + c, 0)),
        out_shape=jax.ShapeDtypeStruct((B * Lp, C_GRP), F32),
        scratch_shapes=[pltpu.VMEM((C_GRP, C_GRP), F32)],
        compiler_params=_cparams(("parallel", "arbitrary")),
        name="hgrn2",
    )(p, lb, ng)


def _rope_tables(Lp, width, head, rd):
    half = rd // 2
    pos = np.arange(Lp, dtype=np.float32)
    inv_freq = jnp.asarray(ROPE_THETA, F32) ** (-jnp.arange(half, dtype=F32) / half)
    ang = jnp.asarray(pos)[:, None] * inv_freq[None, :]
    cos, sin = jnp.cos(ang), jnp.sin(ang)
    d = np.arange(width) % head
    fi = np.where(d < half, d, d - half) % half
    is1 = jnp.asarray(d < half)
    is2 = jnp.asarray((d >= half) & (d < rd))
    cosl, sinl = cos[:, fi], sin[:, fi]
    ct = jnp.where(is1 | is2, cosl, 1.0)
    s1 = jnp.where(is1, -sinl, 0.0)
    s2 = jnp.where(is2, sinl, 0.0)
    return ct, s1, s2


def _rope(x, ct, s1, s2, half):
    n = x.shape[1]
    return x * ct + pltpu.roll(x, n - half, axis=1) * s1 + pltpu.roll(x, half, axis=1) * s2


def _dsa_prep_kernel(p_ref, cq_ref, s1q_ref, s2q_ref, ci_ref, s1i_ref, s2i_ref,
                     ck_ref, s1k_ref, s2k_ref, eq_ref, ek_ref,
                     q_out, k_out, v_out, qs_out, ks_out, kw_out):
    cq, s1q, s2q = cq_ref[...], s1q_ref[...], s2q_ref[...]
    q = _rope(p_ref[:, 0:256], cq, s1q, s2q, 8)
    q_out[...] = (q * (HEAD_DIM ** -0.5 * math.log2(math.e))).astype(BF16)
    k_out[...] = _rope(p_ref[:, 256:512], cq, s1q, s2q, 8).astype(BF16)
    v_out[...] = p_ref[:, 512:768].astype(BF16)
    qi = _rope(p_ref[:, 768:1024], ci_ref[...], s1i_ref[...], s2i_ref[...], 4)
    kw = _rope(p_ref[:, 1024:1152], ck_ref[...], s1k_ref[...], s2k_ref[...], 4)
    kw_out[...] = kw
    q1, q2, q3 = _split3(qi)
    qs = jnp.dot(q1, eq_ref[0], preferred_element_type=F32)
    qs += jnp.dot(q2, eq_ref[1], preferred_element_type=F32)
    qs += jnp.dot(q3, eq_ref[2], preferred_element_type=F32)
    qs_out[...] = qs.astype(BF16)
    k1, k2, k3 = _split3(kw)
    ks = jnp.dot(k1, ek_ref[0], preferred_element_type=F32)
    ks += jnp.dot(k2, ek_ref[1], preferred_element_type=F32)
    ks += jnp.dot(k3, ek_ref[2], preferred_element_type=F32)
    ks_out[...] = ks.astype(BF16)


def _split_expanders():
    q_slots = ((0, 1, 3), (2, 4), (5,))
    k_slots = ((0, 2, 5), (1, 4), (3,))
    eq = np.zeros((3, IDX_HEADS * IDX_DIM, IDX_HEADS * 256), np.float32)
    ek = np.zeros((3, 128, 256), np.float32)
    for part in range(3):
        for j in q_slots[part]:
            for h in range(IDX_HEADS):
                for d in range(IDX_DIM):
                    eq[part, h * IDX_DIM + d, h * 256 + j * IDX_DIM + d] = 1.0
        for j in k_slots[part]:
            for d in range(IDX_DIM):
                ek[part, d, j * IDX_DIM + d] = 1.0
    return jnp.asarray(eq, BF16), jnp.asarray(ek, BF16)


def _dsa_prep_call(p, B, Lp, tm):
    T = B * Lp
    nL = Lp // tm
    tq = _rope_tables(Lp, 256, HEAD_DIM, HEAD_DIM // 4)
    ti = _rope_tables(Lp, 256, IDX_DIM, IDX_DIM // 4)
    tk = tuple(jnp.concatenate([t, jnp.full((Lp, 96), fill, F32)], axis=1)
               for t, fill in zip(_rope_tables(Lp, 32, IDX_DIM, IDX_DIM // 4), (1.0, 0.0, 0.0)))
    eq, ek = _split_expanders()
    tab = lambda n: pl.BlockSpec((tm, n), lambda i: (i % nL, 0))
    row = lambda n: pl.BlockSpec((tm, n), lambda i: (i, 0))
    outs = [(256, BF16), (256, BF16), (256, BF16), (IDX_HEADS * 256, BF16), (256, BF16), (128, F32)]
    return pl.pallas_call(
        _dsa_prep_kernel,
        grid=(T // tm,),
        in_specs=[row(W_DSA)] + [tab(256)] * 6 + [tab(128)] * 3
        + [pl.BlockSpec(eq.shape, lambda i: (0, 0, 0)), pl.BlockSpec(ek.shape, lambda i: (0, 0, 0))],
        out_specs=[row(n) for n, _ in outs],
        out_shape=[jax.ShapeDtypeStruct((T, n), dt) for n, dt in outs],
        compiler_params=_cparams(("parallel",)),
        name="dsa_prep",
    )(p, *tq, *ti, *tk, eq, ek)


def _dsa_kernel(q_ref, qs_ref, w_ref, k_ref, ks_ref, vt_ref, o_ref,
                key_ref, k16_ref, acc_ref, s_ref, p_ref, *, topk):
    TQ, KB = DSA_TQ, DSA_KB
    i = pl.program_id(1)
    q0 = i * TQ
    nkb = (q0 + TQ + KB - 1) // KB
    qpos = q0 + lax.broadcasted_iota(jnp.int32, (1, TQ), 1)
    krow = lax.broadcasted_iota(jnp.int32, (KB, 1), 0)
    nt = (((1,), (1,)), ((), ()))

    def kslice(j):
        return pl.ds(pl.multiple_of(j * KB, KB), KB)

    w = w_ref[...] * (IDX_HEADS ** -0.5 * IDX_DIM ** -0.5)

    def idx_body(j, carry):
        ks = ks_ref[kslice(j), :]
        acc = jnp.zeros((KB, TQ), F32)
        for h in range(IDX_HEADS):
            s = lax.dot_general(ks, qs_ref[:, h * 256:(h + 1) * 256], nt, preferred_element_type=F32)
            acc = acc + w[h:h + 1, :] * jnp.maximum(s, 0.0)
        acc = jnp.where(j * KB + krow <= qpos, acc, NEG_INF)
        bits = pltpu.bitcast(acc, jnp.int32)
        key = bits ^ ((bits >> 31) & jnp.int32(0x7FFFFFFF))
        key_ref[kslice(j), :] = key
        k16_ref[kslice(j), :] = (key >> 16).astype(jnp.int16)
        return carry

    lax.fori_loop(0, nkb, idx_body, 0)

    def count_ge_gt(cand):
        def body(j, cnts):
            blk = key_ref[kslice(j), :]
            ge = (blk >= cand).astype(jnp.int32).reshape(KB // 8, 8, TQ)
            gt = (blk > cand).astype(jnp.int32).reshape(KB // 8, 8, TQ)
            return cnts[0] + jnp.sum(ge, axis=0), cnts[1] + jnp.sum(gt, axis=0)
        zero = jnp.zeros((8, TQ), jnp.int32)
        c_ge, c_gt = lax.fori_loop(0, nkb, body, (zero, zero))
        return jnp.sum(c_ge, axis=0, keepdims=True), jnp.sum(c_gt, axis=0, keepdims=True)

    def count_ge16(cand):
        c16 = cand.astype(jnp.int16)
        def body(j, cnt):
            hit = (k16_ref[kslice(j), :] >= c16).astype(jnp.int16)
            parts = [hit[r:r + 16] for r in range(0, KB, 16)]
            while len(parts) > 1:
                parts = [a + b for a, b in zip(parts[::2], parts[1::2])]
            return cnt + parts[0]
        cnt = lax.fori_loop(0, nkb, body, jnp.zeros((16, TQ), jnp.int16))
        return jnp.sum(cnt.astype(jnp.int32), axis=0, keepdims=True)

    def select16(want):
        def bit_body(b, res_u):
            cand_u = res_u | (jnp.int32(1) << (15 - b))
            cnt = count_ge16(cand_u - HALF)
            return jnp.where(cnt >= want, cand_u, res_u)
        return lax.fori_loop(0, 16, bit_body, jnp.zeros((1, TQ), jnp.int32))

    HALF = 32768
    hi = select16(topk) - HALF
    cnt_above = jnp.where(hi == HALF - 1, 0, count_ge16(jnp.minimum(hi + 1, HALF - 1)))
    hi16 = hi.astype(jnp.int16)

    def low_body(j, carry):
        lo = ((key_ref[kslice(j), :] & jnp.int32(0xFFFF)) - HALF).astype(jnp.int16)
        k16_ref[kslice(j), :] = jnp.where(k16_ref[kslice(j), :] == hi16, lo, jnp.int16(-HALF))
        return carry

    lax.fori_loop(0, nkb, low_body, 0)
    thr = (hi << 16) | select16(topk - cnt_above)
    cnt_ge, cnt_gt = count_ge_gt(thr)
    n_take = topk - cnt_gt
    neg_key = jnp.int32(np.array(NEG_INF, np.float32).view(np.int32)) ^ jnp.int32(0x7FFFFFFF)
    need_rank = jnp.any(((cnt_ge - cnt_gt) > n_take) & (thr > neg_key))

    def put_bias(j, sel):
        key_ref[kslice(j), :] = pltpu.bitcast(jnp.where(sel, 0.0, NEG_INF), jnp.int32)

    def fast_body(j, carry):
        put_bias(j, (key_ref[kslice(j), :] >= thr) & (j * KB + krow <= qpos))
        return carry

    def rank_body(j, seen):
        blk = key_ref[kslice(j), :]
        eq = blk == thr
        ri = lax.broadcasted_iota(jnp.int32, (KB, KB), 0)
        ci = lax.broadcasted_iota(jnp.int32, (KB, KB), 1)
        lower = (ci <= ri).astype(BF16)
        rank = seen + jnp.dot(lower, eq.astype(BF16), preferred_element_type=F32)
        put_bias(j, ((blk > thr) | (eq & (rank <= n_take.astype(F32)))) & (j * KB + krow <= qpos))
        return seen + jnp.sum(eq.astype(F32), axis=0, keepdims=True)

    @pl.when(jnp.logical_not(need_rank))
    def _():
        lax.fori_loop(0, nkb, fast_body, 0)

    @pl.when(need_rank)
    def _():
        lax.fori_loop(0, nkb, rank_body, jnp.zeros((1, TQ), F32))

    lane_head = lax.broadcasted_iota(jnp.int32, (1, C_GRP), 1) // HEAD_DIM
    q = q_ref[...]
    qh = [jnp.where(lane_head == h, q, jnp.zeros_like(q)) for h in range(H_GRP)]

    def scores(j, h):
        bias = pltpu.bitcast(key_ref[kslice(j), :], F32)
        return lax.dot_general(k_ref[kslice(j), :], qh[h], nt, preferred_element_type=F32) + bias

    acc_ref[...] = jnp.zeros(acc_ref.shape, F32)

    def pv_body(j, carry):
        ms, ls = carry
        for h in range(H_GRP):
            s_ref[h] = scores(j, h)
        new_ms, new_ls = [], []
        for h in range(H_GRP):
            s = s_ref[h]
            m_new = jnp.maximum(ms[h], jnp.max(s, axis=0, keepdims=True))
            alpha = jnp.exp2(ms[h] - m_new)
            pexp = jnp.exp2(s - m_new)
            new_ms.append(m_new)
            new_ls.append(ls[h] * alpha + jnp.sum(pexp.reshape(KB // 8, 8, TQ), axis=0))
            p_ref[h] = pexp.astype(BF16)
            hs = slice(h * HEAD_DIM, (h + 1) * HEAD_DIM)
            acc_ref[hs, :] = acc_ref[hs, :] * alpha
        for h in range(H_GRP):
            hs = slice(h * HEAD_DIM, (h + 1) * HEAD_DIM)
            acc_ref[hs, :] += jnp.dot(vt_ref[hs, kslice(j)], p_ref[h], preferred_element_type=F32)
        return tuple(new_ms), tuple(new_ls)

    init = (tuple(jnp.full((1, TQ), 0.1 * NEG_INF, F32) for _ in range(H_GRP)),
            tuple(jnp.zeros((8, TQ), F32) for _ in range(H_GRP)))
    _, ls = lax.fori_loop(0, nkb, pv_body, init)
    for h in range(H_GRP):
        hs = slice(h * HEAD_DIM, (h + 1) * HEAD_DIM)
        acc_ref[hs, :] = acc_ref[hs, :] / jnp.sum(ls[h], axis=0, keepdims=True)
    o_ref[...] = acc_ref[...].T


def _dsa_call(q, qs, wt, k3, ks3, vt3, B, Lp, topk):
    TQ = DSA_TQ
    nQ = Lp // TQ
    rowq = lambda n: pl.BlockSpec((TQ, n), lambda b, i: (b * nQ + i, 0))
    full = lambda r, c: pl.BlockSpec((None, r, c), lambda b, i: (b, 0, 0))
    return pl.pallas_call(
        functools.partial(_dsa_kernel, topk=topk),
        grid=(B, nQ),
        in_specs=[rowq(256), rowq(IDX_HEADS * 256),
                  pl.BlockSpec((None, IDX_HEADS, TQ), lambda b, i: (b, 0, i)),
                  full(Lp, 256), full(Lp, 256), full(256, Lp)],
        out_specs=rowq(C_GRP),
        out_shape=jax.ShapeDtypeStruct((B * Lp, C_GRP), F32),
        scratch_shapes=[pltpu.VMEM((Lp, TQ), jnp.int32),
                        pltpu.VMEM((Lp, TQ), jnp.int16),
                        pltpu.VMEM((C_GRP, TQ), F32),
                        pltpu.VMEM((H_GRP, DSA_KB, TQ), F32),
                        pltpu.VMEM((H_GRP, DSA_KB, TQ), BF16)],
        compiler_params=_cparams(("parallel", "arbitrary")),
        name="dsa_attn",
    )(q, qs, wt, k3, ks3, vt3)


MOE_TILE = 2048
MOE_SUB = 512
MOE_GRP = 128
MOE_FFN = 128
MOE_ALIGN = 16


def _moe_kernel(x_ref, gt_ref, w1_ref, b1_ref, w2_ref, b2_ref, o_ref,
                rank_ref, xc_ref, yc_ref, gc_ref):
    e = pl.program_id(1)
    Tt = x_ref.shape[0]
    SUB = min(MOE_SUB, Tt)
    NS = Tt // SUB
    GRP = MOE_GRP

    @pl.when(jnp.logical_and(pl.program_id(0) == 0, e == 0))
    def _():
        xc_ref[...] = jnp.zeros_like(xc_ref)
        yc_ref[...] = jnp.zeros_like(yc_ref)
        gc_ref[...] = jnp.zeros_like(gc_ref)

    @pl.when(e == 0)
    def _():
        o_ref[...] = jnp.zeros_like(o_ref)
        ri = lax.broadcasted_iota(jnp.int32, (SUB, SUB), 0)
        ci = lax.broadcasted_iota(jnp.int32, (SUB, SUB), 1)
        upper = (ri <= ci).astype(BF16)
        for s in range(NS):
            hot = (gt_ref[:, s * SUB:(s + 1) * SUB] > 0.0).astype(BF16)
            rank_ref[:, s * SUB:(s + 1) * SUB] = jnp.dot(hot, upper, preferred_element_type=F32)

    riota = lax.broadcasted_iota(jnp.int32, (GRP, 1), 0).astype(F32)

    def onehot(s, g):
        cols = slice(s * SUB, (s + 1) * SUB)
        gate = gt_ref[pl.ds(e, 1), cols]
        rank = rank_ref[pl.ds(e, 1), cols]
        pick = (rank == riota + (g * GRP + 1).astype(F32)) & (gate > 0.0)
        return pick, gate

    def count(s):
        cols = slice(s * SUB, (s + 1) * SUB)
        return jnp.max(rank_ref[pl.ds(e, 1), cols]).astype(jnp.int32)

    off = jnp.int32(0)
    offs = []
    for s in range(NS):
        n = count(s)
        offs.append(off)

        def gather(g, carry, s=s, off=off):
            pick, gate = onehot(s, g)
            rows = jnp.dot(pick.astype(BF16), x_ref[s * SUB:(s + 1) * SUB, :],
                           preferred_element_type=F32)
            dst = pl.ds(pl.multiple_of(off + g * GRP, MOE_ALIGN), GRP)
            xc_ref[dst, :] = rows.astype(BF16)
            gc_ref[dst, :] = jnp.broadcast_to(
                jnp.sum(jnp.where(pick, gate, 0.0), axis=1, keepdims=True), (GRP, 128))
            return carry

        lax.fori_loop(0, (n + GRP - 1) // GRP, gather, 0)
        off = off + ((n + MOE_ALIGN - 1) // MOE_ALIGN) * MOE_ALIGN

    DE = D_MODEL

    def ffn(c, carry):
        rows = pl.ds(pl.multiple_of(c * MOE_FFN, MOE_FFN), MOE_FFN)
        u = jnp.dot(xc_ref[rows, :], w1_ref[0], preferred_element_type=F32) + b1_ref[0]
        glu = jnp.minimum(u[:, :DE], SWIGLU_LIMIT)
        lin = jnp.clip(u[:, DE:], -SWIGLU_LIMIT, SWIGLU_LIMIT)
        act = glu * _sigmoid(SWIGLU_ALPHA * glu) * (lin + 1.0)
        y = jnp.dot(act.astype(BF16), w2_ref[0], preferred_element_type=F32) + b2_ref[0]
        yc_ref[rows, :] = (y * gc_ref[rows, 0:1]).astype(BF16)
        return carry

    lax.fori_loop(0, (off + MOE_FFN - 1) // MOE_FFN, ffn, 0)

    for s in range(NS):
        n = count(s)

        def scatter(g, carry, s=s, off=offs[s]):
            pick, _ = onehot(s, g)
            src = pl.ds(pl.multiple_of(off + g * GRP, MOE_ALIGN), GRP)
            o_ref[s * SUB:(s + 1) * SUB, :] += lax.dot_general(
                pick.astype(BF16), yc_ref[src, :], (((0,), (0,)), ((), ())), preferred_element_type=F32)
            return carry

        lax.fori_loop(0, (n + GRP - 1) // GRP, scatter, 0)


def _moe_call(xb, gt, w1, b1, w2, b2, Tt):
    T, D = xb.shape
    E = N_EXPERTS
    cap = Tt + (Tt // MOE_SUB) * MOE_ALIGN + 2 * MOE_GRP
    return pl.pallas_call(
        _moe_kernel,
        grid=(T // Tt, E),
        in_specs=[pl.BlockSpec((Tt, D), lambda i, e: (i, 0)),
                  pl.BlockSpec((E, Tt), lambda i, e: (0, i)),
                  pl.BlockSpec((1, D, 2 * D), lambda i, e: (e, 0, 0)),
                  pl.BlockSpec((1, 1, 2 * D), lambda i, e: (e, 0, 0)),
                  pl.BlockSpec((1, D, D), lambda i, e: (e, 0, 0)),
                  pl.BlockSpec((1, 1, D), lambda i, e: (e, 0, 0))],
        out_specs=pl.BlockSpec((Tt, D), lambda i, e: (i, 0)),
        out_shape=jax.ShapeDtypeStruct((T, D), F32),
        scratch_shapes=[pltpu.VMEM((E, Tt), F32), pltpu.VMEM((cap, D), BF16),
                        pltpu.VMEM((cap, D), BF16), pltpu.VMEM((cap, 128), F32)],
        compiler_params=_cparams(("arbitrary", "arbitrary")),
        name="moe",
    )(xb, gt, w1, b1.reshape(E, 1, 2 * D), w2, b2.reshape(E, 1, D))


def _ln_res_kernel(h_ref, f_ref, g_ref, b_ref, o_ref):
    o_ref[...] = _layernorm_rows(DEEPNORM_ALPHA * h_ref[...] + f_ref[...], g_ref[...], b_ref[...])


def _ln_res_call(h, f, g, b, tm):
    T, D = h.shape
    return pl.pallas_call(
        _ln_res_kernel,
        grid=(T // tm,),
        in_specs=[pl.BlockSpec((tm, D), lambda i: (i, 0)),
                  pl.BlockSpec((tm, D), lambda i: (i, 0)),
                  pl.BlockSpec((1, D), lambda i: (0, 0)),
                  pl.BlockSpec((1, D), lambda i: (0, 0))],
        out_specs=pl.BlockSpec((tm, D), lambda i: (i, 0)),
        out_shape=jax.ShapeDtypeStruct((T, D), F32),
        compiler_params=_cparams(("parallel",)),
        name="ln_ffn",
    )(h, f, g.reshape(1, D), b.reshape(1, D))


def _regroup_w_in(w):
    D = w.shape[0]
    z = lambda n: jnp.zeros((D, n), w.dtype)
    o = 0
    r, wd, k, v, ad, gd = (w[:, a:b] for a, b in _bounds(o, (256, 64, 256, 256, 64, 64)))
    o += 960
    dsa = w[:, o:o + 1064]
    o += 1064
    lru = w[:, o:o + 512]
    o += 512
    hg = w[:, o:o + 1024]
    return jnp.concatenate([r, k, v, wd, ad, gd, z(64), dsa, z(W_DSA - 1064), lru, hg], axis=1).astype(BF16)


def _bounds(start, sizes):
    out = []
    for s in sizes:
        out.append((start, start + s))
        start += s
    return out


def _regroup_mu(mu):
    r, wd, k, v, ad, gd = (mu[a:b] for a, b in _bounds(0, (256, 64, 256, 256, 64, 64)))
    return jnp.concatenate([r, k, v, wd, ad, gd, jnp.zeros((64,), mu.dtype)]).reshape(1, W_RWKV)


def _lora_pad(w, slot):
    return jnp.zeros((C_GRP, C_GRP), w.dtype).at[slot * LORA:(slot + 1) * LORA].set(w).astype(BF16)


def _block_diag(w):
    out = jnp.zeros((C_GRP, C_GRP), w.dtype)
    for h in range(H_GRP):
        out = out.at[h * HEAD_DIM:(h + 1) * HEAD_DIM, h * HEAD_DIM:(h + 1) * HEAD_DIM].set(w[h])
    return out.astype(BF16)


def _pick_moe_tile(T):
    for tt in (MOE_TILE, MOE_TILE // 2, MOE_TILE // 4):
        if T % tt == 0:
            return tt
    raise ValueError(f"no MoE tile for {T}")


def _pick_tile(T, pref):
    for tm in (pref, 512, 256, 128, 64, 32, 16, 8):
        if tm <= pref and T % tm == 0:
            return tm
    raise ValueError(f"no row tile for {T}")


def kernel(x, meta_tokens, ln_in_g, ln_in_b, w_in, rwkv_mu, rwkv_w0, rwkv_w2, rwkv_a0, rwkv_a2,
           rwkv_g2, rwkv_k_k, rwkv_k_a, rwkv_r_k, rwkv_ln_g, rwkv_ln_b, lru_conv_w, lru_conv_b,
           lru_wa, lru_ba, lru_wx, lru_bx, lru_lambda, hgrn_lower_bounds, hgrn_norm_g, w_out,
           ln_mix_g, ln_mix_b, router_w, router_b, moe_w1, moe_b1, moe_w2, moe_b2,
           ln_ffn_g, ln_ffn_b):
    B, S, D = x.shape
    assert D == D_MODEL
    depth = w_in.shape[0]
    topk = min(TOPK_MAX, S // 4)
    L = S + N_META
    Lp = -(-L // SEQ_ALIGN) * SEQ_ALIGN
    T = B * Lp
    tm = _pick_tile(T, 512)
    v1 = lambda t: t.reshape(1, -1)

    h = jnp.concatenate([jnp.broadcast_to(meta_tokens[None].astype(x.dtype), (B, N_META, D)), x,
                         jnp.zeros((B, Lp - L, D), x.dtype)], axis=1).reshape(T, D)
    h = _ln_call(h, ln_in_g, ln_in_b, tm)

    s_lb = jax.nn.softmax(hgrn_lower_bounds.astype(F32), axis=0)
    lower_bounds = jnp.cumsum(s_lb, axis=0) - s_lb[0]

    for l in range(depth):
        p_rwkv, p_dsa, p_lru, p_hgrn = _proj_call(h, _regroup_w_in(w_in[l]), tm)

        y_rwkv = _rwkv_call(p_rwkv, (
            _regroup_mu(rwkv_mu[l]), v1(rwkv_w0[l]), _lora_pad(rwkv_w2[l], 0), v1(rwkv_a0[l]),
            _lora_pad(rwkv_a2[l], 1), _lora_pad(rwkv_g2[l], 2), v1(rwkv_k_k[l]), v1(rwkv_k_a[l]),
            v1(rwkv_r_k[l]), v1(rwkv_ln_g[l]), v1(rwkv_ln_b[l])), B, Lp)

        q, k, v, qs, ks, kw = _dsa_prep_call(p_dsa, B, Lp, _pick_tile(Lp, 512))
        wt = jnp.swapaxes(kw.reshape(B, Lp, 128)[:, :, IDX_DIM:IDX_DIM + IDX_HEADS], 1, 2)
        vt = jnp.swapaxes(v.reshape(B, Lp, 256), 1, 2)
        y_dsa = _dsa_call(q, qs, wt, k.reshape(B, Lp, 256), ks.reshape(B, Lp, 256), vt, B, Lp, topk)

        y_lru = _lru_call(p_lru, (lru_conv_w[l], v1(lru_conv_b[l]), _block_diag(lru_wa[l]),
                                  v1(lru_ba[l]), _block_diag(lru_wx[l]), v1(lru_bx[l]),
                                  v1(lru_lambda[l])), B, Lp)

        y_hgrn = _hgrn_call(p_hgrn, v1(lower_bounds[l]), v1(hgrn_norm_g[l]), B, Lp)

        h, hb, gates_t = _mix_call(h, (y_rwkv, y_dsa, y_lru, y_hgrn), w_out[l].astype(BF16),
                                   ln_mix_g[l], ln_mix_b[l], router_w[l], router_b[l], tm)
        ffn = _moe_call(hb, gates_t, moe_w1[l].astype(BF16), moe_b1[l], moe_w2[l].astype(BF16),
                        moe_b2[l], _pick_moe_tile(T))
        h = _ln_res_call(h, ffn, ln_ffn_g[l], ln_ffn_b[l], tm)

    return h.reshape(B, Lp, D)[:, N_META:L]
```
